```python
import math, functools
import jax, jax.numpy as jnp
from jax import lax
import numpy as np

D_MODEL = 2048
BATCH = 8
SEQ = 4096
DEPTH = 4

N_MIXERS = 3
EPS = 1e-6
CONF_KERNEL_WIDTH = 31
POOL_WINDOWS = (2, 4, 8, 16)
N_POOL_GROUPS = len(POOL_WINDOWS)
POOL_GROUP_DIM = D_MODEL // N_POOL_GROUPS
HEAD_DIM = 64
N_HEADS = D_MODEL // HEAD_DIM
N_KV_HEADS = N_HEADS // 8
GROUP_SIZE = N_HEADS // N_KV_HEADS
WINDOW = 128
BLOCK = 128
ROT_DIM = HEAD_DIM // 4
ROPE_THETA = 500000.0
D_FF = 5632
FFN_CONV_WIDTH = 3

kernel_name = "hybrid_conv_pool_swa_trunk"


def rms_norm(x, g):
    xf = x.astype(jnp.float32)
    y = xf * lax.rsqrt(jnp.mean(xf * xf, axis=-1, keepdims=True) + EPS)
    return (y * g.astype(jnp.float32)).astype(x.dtype)


def layer_norm(x, g, b):
    xf = x.astype(jnp.float32)
    mu = jnp.mean(xf, axis=-1, keepdims=True)
    xc = xf - mu
    var = jnp.mean(xc * xc, axis=-1, keepdims=True)
    y = xc * lax.rsqrt(var + EPS) * g.astype(jnp.float32) + b.astype(jnp.float32)
    return y.astype(x.dtype)


def causal_depthwise_conv(x, w, b):
    width, channels = w.shape
    y = lax.conv_general_dilated(
        x, w[:, None, :].astype(x.dtype), window_strides=(1,), padding=[(width - 1, 0)],
        dimension_numbers=("NWC", "WIO", "NWC"), feature_group_count=channels)
    return y + b.astype(x.dtype)


def conformer_conv_module(h, w_in, b_in, dw_w, dw_b, ln_g, ln_b, w_out, b_out):
    u = h @ w_in + b_in
    a, gate = jnp.split(u, 2, axis=-1)
    u = a * jax.nn.sigmoid(gate)
    u = causal_depthwise_conv(u, dw_w, dw_b)
    u = jax.nn.silu(layer_norm(u, ln_g, ln_b))
    return u @ w_out + b_out


def multiscale_pool_mixer(h, w_group, scale):
    bsz, seq, dim = h.shape
    hf = h.astype(jnp.float32)
    cs = jnp.concatenate([jnp.zeros((bsz, 1, dim), jnp.float32), jnp.cumsum(hf, axis=1)], axis=1)
    upper = cs[:, 1:]
    t = jnp.arange(seq)
    pooled = []
    for g, w in enumerate(POOL_WINDOWS):
        sl = slice(g * POOL_GROUP_DIM, (g + 1) * POOL_GROUP_DIM)
        lower = jnp.concatenate(
            [jnp.zeros((bsz, w - 1, POOL_GROUP_DIM), jnp.float32), cs[:, :seq + 1 - w, sl]], axis=1)
        count = jnp.minimum(t + 1, w).astype(jnp.float32)[None, :, None]
        pooled.append((upper[..., sl] - lower) / count)
    mixed = (jnp.concatenate(pooled, axis=-1) - hf).astype(h.dtype)
    mixed = mixed.reshape(bsz, seq, N_POOL_GROUPS, POOL_GROUP_DIM)
    y = jnp.einsum("bsgc,gcd->bsgd", mixed, w_group).reshape(bsz, seq, dim)
    return y * scale


def apply_partial_rotary(x, cos, sin):
    half = ROT_DIM // 2
    xr = x[..., :ROT_DIM].astype(jnp.float32)
    x1, x2 = xr[..., :half], xr[..., half:]
    rot = jnp.concatenate([x1 * cos - x2 * sin, x2 * cos + x1 * sin], axis=-1)
    return jnp.concatenate([rot.astype(x.dtype), x[..., ROT_DIM:]], axis=-1)


def banded_sink_attention(q, k, v, sinks):
    bsz, seq = q.shape[:2]
    nb = seq // BLOCK
    scale = 1.0 / math.sqrt(HEAD_DIM)
    qb = q.reshape(bsz, nb, BLOCK, N_KV_HEADS, GROUP_SIZE, HEAD_DIM).transpose(1, 0, 3, 4, 2, 5)

    def band(t):
        tb = t.reshape(bsz, nb, BLOCK, N_KV_HEADS, HEAD_DIM)
        prev = jnp.concatenate([jnp.zeros_like(tb[:, :1]), tb[:, :-1]], axis=1)
        return jnp.concatenate([prev, tb], axis=2).transpose(1, 0, 3, 2, 4)

    kb, vb = band(k), band(v)
    qi = jnp.arange(BLOCK)[:, None]
    kj = jnp.arange(2 * BLOCK)[None, :]
    diff = qi + BLOCK - kj
    in_window = (diff >= 0) & (diff < WINDOW)
    sink = sinks.astype(jnp.float32).reshape(N_KV_HEADS, GROUP_SIZE)[None, :, :, None, None]

    def block_fn(args):
        n, qn, kn, vn = args
        s = jnp.einsum("bkgqd,bkjd->bkgqj", qn.astype(jnp.float32), kn.astype(jnp.float32)) * scale
        valid = in_window & ((n * BLOCK + kj - BLOCK) >= 0)
        s = jnp.where(valid, s, -jnp.inf)
        m = jnp.maximum(jnp.max(s, axis=-1, keepdims=True), sink)
        p = jnp.exp(s - m)
        denom = jnp.sum(p, axis=-1, keepdims=True) + jnp.exp(sink - m)
        o = jnp.einsum("bkgqj,bkjd->bkgqd", p, vn.astype(jnp.float32)) / denom
        return o.astype(q.dtype)

    out = lax.map(block_fn, (jnp.arange(nb), qb, kb, vb))
    return out.transpose(1, 0, 4, 2, 3, 5).reshape(bsz, seq, N_HEADS * HEAD_DIM)


def swa_sink_attention(h, positions, w_qkv, q_norm_g, k_norm_g, sinks, w_o):
    bsz, seq, _ = h.shape
    qkv = h @ w_qkv
    q, k, v = jnp.split(qkv, [N_HEADS * HEAD_DIM, (N_HEADS + N_KV_HEADS) * HEAD_DIM], axis=-1)
    q = rms_norm(q.reshape(bsz, seq, N_HEADS, HEAD_DIM), q_norm_g)
    k = rms_norm(k.reshape(bsz, seq, N_KV_HEADS, HEAD_DIM), k_norm_g)
    v = v.reshape(bsz, seq, N_KV_HEADS, HEAD_DIM)
    inv_freq = ROPE_THETA ** (-jnp.arange(0, ROT_DIM, 2, dtype=jnp.float32) / ROT_DIM)
    ang = positions.astype(jnp.float32)[..., None] * inv_freq
    cos, sin = jnp.cos(ang)[:, :, None, :], jnp.sin(ang)[:, :, None, :]
    q = apply_partial_rotary(q, cos, sin)
    k = apply_partial_rotary(k, cos, sin)
    o = banded_sink_attention(q, k, v, sinks)
    return o @ w_o


def conv_gated_mlp(h, w_up, dw_w, dw_b, w_down):
    u = causal_depthwise_conv(h @ w_up, dw_w, dw_b)
    gate, val = jnp.split(u, 2, axis=-1)
    return (jax.nn.silu(gate) * val) @ w_down


def setup_inputs(seed: int = 0) -> dict:
    key = jax.random.key(seed)
    keys = iter(jax.random.split(key, 64))

    def nrm(shape, scale):
        return jax.random.normal(next(keys), shape, jnp.float32) * scale

    def gain(shape):
        return 1.0 + nrm(shape, 0.02)

    d = D_MODEL
    out = {}
    out["x"] = nrm((BATCH, SEQ, d), 1.0)
    offset = jax.random.randint(next(keys), (BATCH, 1), 0, 4096, dtype=jnp.int32)
    out["positions"] = (jnp.arange(SEQ, dtype=jnp.int32)[None, :] + offset).astype(jnp.int32)

    def add_conformer(p):
        out[p + "norm_g"] = gain((d,))
        out[p + "a_w_in"] = nrm((d, 2 * d), d ** -0.5)
        out[p + "a_b_in"] = nrm((2 * d,), 0.02)
        out[p + "a_dw_w"] = nrm((CONF_KERNEL_WIDTH, d), CONF_KERNEL_WIDTH ** -0.5)
        out[p + "a_dw_b"] = nrm((d,), 0.02)
        out[p + "a_ln_g"] = gain((d,))
        out[p + "a_ln_b"] = nrm((d,), 0.02)
        out[p + "a_w_out"] = nrm((d, d), d ** -0.5)
        out[p + "a_b_out"] = nrm((d,), 0.02)

    def add_ffn(p):
        out[p + "ffn_norm_g"] = gain((d,))
        out[p + "ffn_w_up"] = nrm((d, 2 * D_FF), d ** -0.5)
        out[p + "ffn_dw_w"] = nrm((FFN_CONV_WIDTH, 2 * D_FF), FFN_CONV_WIDTH ** -0.5)
        out[p + "ffn_dw_b"] = nrm((2 * D_FF,), 0.02)
        out[p + "ffn_w_down"] = nrm((D_FF, d), D_FF ** -0.5)

    add_conformer("l0_")
    add_ffn("l0_")
    out["l1_norm_g"] = gain((d,))
    out["l1_b_w_group"] = nrm((N_POOL_GROUPS, POOL_GROUP_DIM, POOL_GROUP_DIM), POOL_GROUP_DIM ** -0.5)
    out["l1_b_scale"] = 1.0 + nrm((d,), 0.1)
    add_ffn("l1_")
    out["l2_norm_g"] = gain((d,))
    out["l2_c_w_qkv"] = nrm((d, (N_HEADS + 2 * N_KV_HEADS) * HEAD_DIM), d ** -0.5)
    out["l2_c_q_norm_g"] = gain((HEAD_DIM,))
    out["l2_c_k_norm_g"] = gain((HEAD_DIM,))
    out["l2_c_sinks"] = nrm((N_HEADS,), 1.0)
    out["l2_c_w_o"] = nrm((N_HEADS * HEAD_DIM, d), (N_HEADS * HEAD_DIM) ** -0.5)
    add_ffn("l2_")
    add_conformer("l3_")
    add_ffn("l3_")
    return out


def reference(x, positions,
              l0_norm_g, l0_a_w_in, l0_a_b_in, l0_a_dw_w, l0_a_dw_b, l0_a_ln_g, l0_a_ln_b, l0_a_w_out, l0_a_b_out,
              l0_ffn_norm_g, l0_ffn_w_up, l0_ffn_dw_w, l0_ffn_dw_b, l0_ffn_w_down,
              l1_norm_g, l1_b_w_group, l1_b_scale,
              l1_ffn_norm_g, l1_ffn_w_up, l1_ffn_dw_w, l1_ffn_dw_b, l1_ffn_w_down,
              l2_norm_g, l2_c_w_qkv, l2_c_q_norm_g, l2_c_k_norm_g, l2_c_sinks, l2_c_w_o,
              l2_ffn_norm_g, l2_ffn_w_up, l2_ffn_dw_w, l2_ffn_dw_b, l2_ffn_w_down,
              l3_norm_g, l3_a_w_in, l3_a_b_in, l3_a_dw_w, l3_a_dw_b, l3_a_ln_g, l3_a_ln_b, l3_a_w_out, l3_a_b_out,
              l3_ffn_norm_g, l3_ffn_w_up, l3_ffn_dw_w, l3_ffn_dw_b, l3_ffn_w_down):
    mixers = [
        lambda h: conformer_conv_module(h, l0_a_w_in, l0_a_b_in, l0_a_dw_w, l0_a_dw_b,
                                        l0_a_ln_g, l0_a_ln_b, l0_a_w_out, l0_a_b_out),
        lambda h: multiscale_pool_mixer(h, l1_b_w_group, l1_b_scale),
        lambda h: swa_sink_attention(h, positions, l2_c_w_qkv, l2_c_q_norm_g, l2_c_k_norm_g,
                                     l2_c_sinks, l2_c_w_o),
        lambda h: conformer_conv_module(h, l3_a_w_in, l3_a_b_in, l3_a_dw_w, l3_a_dw_b,
                                        l3_a_ln_g, l3_a_ln_b, l3_a_w_out, l3_a_b_out),
    ]
    mixer_norms = [l0_norm_g, l1_norm_g, l2_norm_g, l3_norm_g]
    ffns = [
        (l0_ffn_norm_g, l0_ffn_w_up, l0_ffn_dw_w, l0_ffn_dw_b, l0_ffn_w_down),
        (l1_ffn_norm_g, l1_ffn_w_up, l1_ffn_dw_w, l1_ffn_dw_b, l1_ffn_w_down),
        (l2_ffn_norm_g, l2_ffn_w_up, l2_ffn_dw_w, l2_ffn_dw_b, l2_ffn_w_down),
        (l3_ffn_norm_g, l3_ffn_w_up, l3_ffn_dw_w, l3_ffn_dw_b, l3_ffn_w_down),
    ]
    for i in range(DEPTH):
        x = x + mixers[i](rms_norm(x, mixer_norms[i]))
        g, w_up, dw_w, dw_b, w_down = ffns[i]
        x = x + conv_gated_mlp(rms_norm(x, g), w_up, dw_w, dw_b, w_down)
    return x
```

```python
import functools

import jax
import jax.numpy as jnp
from jax import lax
from jax.experimental import pallas as pl
from jax.experimental.pallas import tpu as pltpu

F32 = jnp.float32
BF16 = jnp.bfloat16

EPS = 1e-6
LANES = 128
HEAD_DIM = 64
N_HEADS = 32
N_KV_HEADS = 4
GROUP_SIZE = N_HEADS // N_KV_HEADS
ATT_BLOCK = 128
ROT_DIM = 16
ROPE_THETA = 500000.0
POOL_WINDOWS = (2, 4, 8, 16)
CONF_WIDTH = 31
FFN_CONV_WIDTH = 3
CONF_HALO = 32
POOL_HALO = 16
FFN_HALO = 8
NEG_BIG = -1e30

VMEM_LIMIT_BYTES = 56 * 1024 * 1024


def _params(*semantics):
    return pltpu.CompilerParams(dimension_semantics=semantics, vmem_limit_bytes=VMEM_LIMIT_BYTES)


def _sigmoid(x):
    return 1.0 / (1.0 + jnp.exp(-x))


def _rms_scale(x, g):
    ms = jnp.mean(x * x, axis=-1, keepdims=True)
    return x * lax.rsqrt(ms + EPS) * g


def _dot(a, b):
    return jnp.dot(a, b, preferred_element_type=F32)


def _row(v):
    return v.reshape(1, -1)


def _prenorm_kernel(x_ref, g_ref, hn_ref):
    hn_ref[...] = _rms_scale(x_ref[...], g_ref[...]).astype(BF16)


def _prenorm(x, g, tm=1024):
    n, d = x.shape
    return pl.pallas_call(
        _prenorm_kernel,
        grid=(n // tm,),
        in_specs=[pl.BlockSpec((tm, d), lambda i: (i, 0)), pl.BlockSpec((1, d), lambda i: (0, 0))],
        out_specs=pl.BlockSpec((tm, d), lambda i: (i, 0)),
        out_shape=jax.ShapeDtypeStruct((n, d), BF16),
        compiler_params=_params("arbitrary"),
        name="prenorm",
    )(x, _row(g))


def _glu_kernel(hn_ref, wa_ref, wg_ref, ba_ref, bg_ref, o_ref):
    hn = hn_ref[...]
    a = _dot(hn, wa_ref[...]) + ba_ref[...]
    g = _dot(hn, wg_ref[...]) + bg_ref[...]
    o_ref[...] = (a * _sigmoid(g)).astype(BF16)


def _conformer_glu(hn, w_in, b_in, tm=1024, tn=1024):
    n, d = hn.shape
    nb = d // tn
    b_in = _row(b_in)
    return pl.pallas_call(
        _glu_kernel,
        grid=(n // tm, nb),
        in_specs=[
            pl.BlockSpec((tm, d), lambda i, j: (i, 0)),
            pl.BlockSpec((d, tn), lambda i, j: (0, j)),
            pl.BlockSpec((d, tn), lambda i, j: (0, j + nb)),
            pl.BlockSpec((1, tn), lambda i, j: (0, j)),
            pl.BlockSpec((1, tn), lambda i, j: (0, j + nb)),
        ],
        out_specs=pl.BlockSpec((tm, tn), lambda i, j: (i, j)),
        out_shape=jax.ShapeDtypeStruct((n, d), BF16),
        compiler_params=_params("arbitrary", "arbitrary"),
        name="conformer_glu",
    )(hn, w_in, w_in, b_in, b_in)


def _conformer_out_kernel(glu_ref, halo_ref, x_ref, dww_ref, dwb_ref, lng_ref, lnb_ref, wo_ref, bo_ref, gn_ref,
                          xo_ref, hn_ref, s_ref, y_ref, *, tiles_per_seq, row_chunk):
    tm, d = glu_ref.shape
    ncol = d // LANES
    first = (pl.program_id(0) % tiles_per_seq) == 0
    halo = jnp.where(first, 0.0, halo_ref[...].astype(F32))
    cur = glu_ref[...].astype(F32)
    for c in range(ncol):
        s_ref[c, 0:CONF_HALO, :] = halo[:, c * LANES:(c + 1) * LANES]
        s_ref[c, CONF_HALO:CONF_HALO + tm, :] = cur[:, c * LANES:(c + 1) * LANES]

    base = CONF_HALO - (CONF_WIDTH - 1)

    def col_body(c, carry):
        bias = dwb_ref[c]
        for r in range(tm // row_chunk):
            acc = jnp.broadcast_to(bias, (row_chunk, LANES))
            for j in range(CONF_WIDTH):
                acc = acc + dww_ref[c, j:j + 1, :] * s_ref[c, pl.ds(r * row_chunk + base + j, row_chunk), :]
            y_ref[c, r * row_chunk:(r + 1) * row_chunk, :] = acc
        return carry

    lax.fori_loop(0, ncol, col_body, 0)

    y = jnp.concatenate([y_ref[c] for c in range(ncol)], axis=1)
    mu = jnp.mean(y, axis=-1, keepdims=True)
    yc = y - mu
    var = jnp.mean(yc * yc, axis=-1, keepdims=True)
    z = yc * lax.rsqrt(var + EPS) * lng_ref[...] + lnb_ref[...]
    a = (z * _sigmoid(z)).astype(BF16)
    out = x_ref[...] + (_dot(a, wo_ref[...]) + bo_ref[...])
    xo_ref[...] = out
    hn_ref[...] = _rms_scale(out, gn_ref[...]).astype(BF16)


def _conformer_out(glu, x, dw_w, dw_b, ln_g, ln_b, w_out, b_out, g_next, seq, tm=512, row_chunk=64):
    n, d = x.shape
    ncol = d // LANES
    halo_blocks = tm // CONF_HALO
    dww = dw_w.reshape(CONF_WIDTH, ncol, LANES).transpose(1, 0, 2)
    dwb = dw_b.reshape(ncol, 1, LANES)
    kern = functools.partial(_conformer_out_kernel, tiles_per_seq=seq // tm, row_chunk=row_chunk)
    vec = pl.BlockSpec((1, d), lambda i: (0, 0))
    return pl.pallas_call(
        kern,
        grid=(n // tm,),
        in_specs=[
            pl.BlockSpec((tm, d), lambda i: (i, 0)),
            pl.BlockSpec((CONF_HALO, d), lambda i: (jnp.maximum(i * halo_blocks - 1, 0), 0)),
            pl.BlockSpec((tm, d), lambda i: (i, 0)),
            pl.BlockSpec((ncol, CONF_WIDTH, LANES), lambda i: (0, 0, 0)),
            pl.BlockSpec((ncol, 1, LANES), lambda i: (0, 0, 0)),
            vec, vec,
            pl.BlockSpec((d, d), lambda i: (0, 0)),
            vec, vec,
        ],
        out_specs=[pl.BlockSpec((tm, d), lambda i: (i, 0)), pl.BlockSpec((tm, d), lambda i: (i, 0))],
        out_shape=[jax.ShapeDtypeStruct((n, d), F32), jax.ShapeDtypeStruct((n, d), BF16)],
        scratch_shapes=[pltpu.VMEM((ncol, tm + CONF_HALO, LANES), F32), pltpu.VMEM((ncol, tm, LANES), F32)],
        compiler_params=_params("arbitrary"),
        name="conformer_out",
    )(glu, glu, x, dww, dwb, _row(ln_g), _row(ln_b), w_out, _row(b_out), _row(g_next))


def _ffn_kernel(hn_ref, x_ref, wg_ref, wv_ref, cwg_ref, cwv_ref, cbg_ref, cbv_ref, wd_ref, gn_ref,
                xo_ref, hn_out_ref, acc_ref, sg_ref, sv_ref, carry_g_ref, carry_v_ref, *, tiles_per_seq):
    tm = hn_ref.shape[0]
    j = pl.program_id(1)
    nj = pl.num_programs(1)
    first = (pl.program_id(0) % tiles_per_seq) == 0

    hn = hn_ref[...]

    def conv_branch(w_ref, s_ref, carry_ref, cw_ref, cb_ref):
        u = _dot(hn, w_ref[...])
        s_ref[0:FFN_HALO, :] = jnp.where(first, 0.0, carry_ref[j])
        s_ref[FFN_HALO:FFN_HALO + tm, :] = u
        carry_ref[j] = u[tm - FFN_HALO:, :]
        out = cb_ref[...] + cw_ref[FFN_CONV_WIDTH - 1:FFN_CONV_WIDTH, :] * u
        for k in range(1, FFN_CONV_WIDTH):
            out = out + cw_ref[FFN_CONV_WIDTH - 1 - k:FFN_CONV_WIDTH - k, :] * s_ref[pl.ds(FFN_HALO - k, tm), :]
        return out

    gate = conv_branch(wg_ref, sg_ref, carry_g_ref, cwg_ref, cbg_ref)
    val = conv_branch(wv_ref, sv_ref, carry_v_ref, cwv_ref, cbv_ref)
    p = (gate * _sigmoid(gate) * val).astype(BF16)
    contrib = _dot(p, wd_ref[...])

    @pl.when(j == 0)
    def _():
        acc_ref[...] = contrib

    @pl.when(j > 0)
    def _():
        acc_ref[...] += contrib

    @pl.when(j == nj - 1)
    def _():
        out = x_ref[...] + acc_ref[...]
        xo_ref[...] = out
        hn_out_ref[...] = _rms_scale(out, gn_ref[...]).astype(BF16)


def _ffn(hn, x, w_up, dw_w, dw_b, w_down, g_next, seq, tm=512, tf=512):
    n, d = x.shape
    dff = w_down.shape[0]
    nj = dff // tf
    dw_b = _row(dw_b)
    kern = functools.partial(_ffn_kernel, tiles_per_seq=seq // tm)
    return pl.pallas_call(
        kern,
        grid=(n // tm, nj),
        in_specs=[
            pl.BlockSpec((tm, d), lambda i, j: (i, 0)),
            pl.BlockSpec((tm, d), lambda i, j: (i, 0)),
            pl.BlockSpec((d, tf), lambda i, j: (0, j)),
            pl.BlockSpec((d, tf), lambda i, j: (0, j + nj)),
            pl.BlockSpec((FFN_CONV_WIDTH, tf), lambda i, j: (0, j)),
            pl.BlockSpec((FFN_CONV_WIDTH, tf), lambda i, j: (0, j + nj)),
            pl.BlockSpec((1, tf), lambda i, j: (0, j)),
            pl.BlockSpec((1, tf), lambda i, j: (0, j + nj)),
            pl.BlockSpec((tf, d), lambda i, j: (j, 0)),
            pl.BlockSpec((1, d), lambda i, j: (0, 0)),
        ],
        out_specs=[pl.BlockSpec((tm, d), lambda i, j: (i, 0)), pl.BlockSpec((tm, d), lambda i, j: (i, 0))],
        out_shape=[jax.ShapeDtypeStruct((n, d), F32), jax.ShapeDtypeStruct((n, d), BF16)],
        scratch_shapes=[
            pltpu.VMEM((tm, d), F32),
            pltpu.VMEM((tm + FFN_HALO, tf), F32),
            pltpu.VMEM((tm + FFN_HALO, tf), F32),
            pltpu.VMEM((nj, FFN_HALO, tf), F32),
            pltpu.VMEM((nj, FFN_HALO, tf), F32),
        ],
        compiler_params=_params("arbitrary", "arbitrary"),
        name="conv_gated_mlp",
    )(hn, x, w_up, w_up, dw_w, dw_w, dw_b, dw_b, w_down, _row(g_next))


def _pool_kernel(hn_ref, halo_ref, x_ref, wgrp_ref, scale_ref, gn_ref, xo_ref, hn_out_ref, s_ref, m_ref,
                 *, tiles_per_seq, row_chunk):
    tm, d = hn_ref.shape
    gdim = d // len(POOL_WINDOWS)
    tile = pl.program_id(0) % tiles_per_seq
    first = tile == 0
    s_ref[0:POOL_HALO, :] = jnp.where(first, 0.0, halo_ref[...].astype(F32))
    s_ref[POOL_HALO:POOL_HALO + tm, :] = hn_ref[...].astype(F32)

    t = tile * tm + lax.broadcasted_iota(jnp.int32, (tm, 1), 0)
    for g, w in enumerate(POOL_WINDOWS):
        cols = slice(g * gdim, (g + 1) * gdim)
        inv_count = 1.0 / jnp.minimum(t + 1, w).astype(F32)
        for r in range(tm // row_chunk):
            rows = r * row_chunk
            cur = s_ref[POOL_HALO + rows:POOL_HALO + rows + row_chunk, cols]
            acc = cur
            for k in range(1, w):
                acc = acc + s_ref[pl.ds(POOL_HALO + rows - k, row_chunk), cols]
            mixed = acc * inv_count[rows:rows + row_chunk] - cur
            m_ref[rows:rows + row_chunk, cols] = mixed.astype(BF16)

    y = jnp.concatenate(
        [_dot(m_ref[:, g * gdim:(g + 1) * gdim], wgrp_ref[g]) for g in range(len(POOL_WINDOWS))], axis=1)
    out = x_ref[...] + y * scale_ref[...]
    xo_ref[...] = out
    hn_out_ref[...] = _rms_scale(out, gn_ref[...]).astype(BF16)


def _pool_mixer(hn, x, w_group, scale, g_next, seq, tm=512, row_chunk=64):
    n, d = x.shape
    ng, gdim, _ = w_group.shape
    halo_blocks = tm // POOL_HALO
    kern = functools.partial(_pool_kernel, tiles_per_seq=seq // tm, row_chunk=row_chunk)
    vec = pl.BlockSpec((1, d), lambda i: (0, 0))
    return pl.pallas_call(
        kern,
        grid=(n // tm,),
        in_specs=[
            pl.BlockSpec((tm, d), lambda i: (i, 0)),
            pl.BlockSpec((POOL_HALO, d), lambda i: (jnp.maximum(i * halo_blocks - 1, 0), 0)),
            pl.BlockSpec((tm, d), lambda i: (i, 0)),
            pl.BlockSpec((ng, gdim, gdim), lambda i: (0, 0, 0)),
            vec, vec,
        ],
        out_specs=[pl.BlockSpec((tm, d), lambda i: (i, 0)), pl.BlockSpec((tm, d), lambda i: (i, 0))],
        out_shape=[jax.ShapeDtypeStruct((n, d), F32), jax.ShapeDtypeStruct((n, d), BF16)],
        scratch_shapes=[pltpu.VMEM((tm + POOL_HALO, d), F32), pltpu.VMEM((tm, d), BF16)],
        compiler_params=_params("arbitrary"),
        name="pool_mixer",
    )(hn, hn, x, w_group, _row(scale), _row(g_next))


def _qkv_kernel(hn_ref, pos_ref, w_ref, e_ref, et_ref, gain_ref, invf_ref, q_ref, k_ref, v_ref):
    d_q = q_ref.shape[1]
    d_kv = N_KV_HEADS * HEAD_DIM
    d_qk = d_q + d_kv
    acc = _dot(hn_ref[...], w_ref[...])
    qk = acc[:, :d_qk]

    ms = _dot((qk * qk).astype(BF16), e_ref[...]) * (1.0 / HEAD_DIM)
    scale = lax.rsqrt(ms + EPS)
    scale_hi = scale.astype(BF16)
    scale_lo = (scale - scale_hi.astype(F32)).astype(BF16)
    scale_b = _dot(scale_hi, et_ref[...]) + _dot(scale_lo, et_ref[...])
    qkn = qk * scale_b * gain_ref[...]

    half = ROT_DIM // 2
    lane = lax.broadcasted_iota(jnp.int32, (1, LANES), 1) % HEAD_DIM
    ang = pos_ref[...].astype(F32) * invf_ref[...]
    cos = jnp.cos(ang)
    sin = jnp.sin(ang)
    c_self = jnp.where(lane < ROT_DIM, cos, 1.0)
    c_up = jnp.where(lane < half, -sin, 0.0)
    c_dn = jnp.where((lane >= half) & (lane < ROT_DIM), sin, 0.0)
    cols = []
    for c in range(d_qk // LANES):
        xc = qkn[:, c * LANES:(c + 1) * LANES]
        cols.append(xc * c_self + pltpu.roll(xc, LANES - half, 1) * c_up + pltpu.roll(xc, half, 1) * c_dn)

    nq = d_q // LANES
    sm_scale = 1.0 / (HEAD_DIM ** 0.5)
    q_ref[...] = (jnp.concatenate(cols[:nq], axis=1) * sm_scale).astype(BF16)

    lo = lax.broadcasted_iota(jnp.int32, (1, LANES), 1) < HEAD_DIM

    def dup(xc):
        r = pltpu.roll(xc, HEAD_DIM, 1)
        return [jnp.where(lo, xc, r), jnp.where(lo, r, xc)]

    kd, vd = [], []
    for c in range(d_kv // LANES):
        kd += dup(cols[nq + c])
        vd += dup(acc[:, d_qk + c * LANES:d_qk + (c + 1) * LANES])
    k_ref[...] = jnp.concatenate(kd, axis=1).astype(BF16)
    v_ref[...] = jnp.concatenate(vd, axis=1).astype(BF16)


def _qkv(hn, positions, w_qkv, q_gain, k_gain, tm=512):
    n, d = hn.shape
    d_q = N_HEADS * HEAD_DIM
    d_kv = N_KV_HEADS * HEAD_DIM
    d_qk = d_q + d_kv
    d_all = d_q + 2 * d_kv
    head_of_col = jnp.arange(d_qk) // HEAD_DIM
    e = (head_of_col[:, None] == jnp.arange(LANES)[None, :]).astype(BF16)
    et = e.T
    gain = jnp.concatenate([jnp.tile(q_gain, N_HEADS), jnp.tile(k_gain, N_KV_HEADS)]).astype(F32)
    inv_freq = ROPE_THETA ** (-jnp.arange(0, ROT_DIM, 2, dtype=F32) / ROT_DIM)
    lane = jnp.arange(LANES) % HEAD_DIM
    invf = jnp.where(lane < ROT_DIM, inv_freq[lane % (ROT_DIM // 2)], 0.0).astype(F32)
    const = lambda shape: pl.BlockSpec(shape, lambda i: (0, 0))
    return pl.pallas_call(
        _qkv_kernel,
        grid=(n // tm,),
        in_specs=[
            pl.BlockSpec((tm, d), lambda i: (i, 0)),
            pl.BlockSpec((tm, 1), lambda i: (i, 0)),
            const((d, d_all)), const((d_qk, LANES)), const((LANES, d_qk)), const((1, d_qk)), const((1, LANES)),
        ],
        out_specs=[
            pl.BlockSpec((tm, d_q), lambda i: (i, 0)),
            pl.BlockSpec((tm, 2 * d_kv), lambda i: (i, 0)),
            pl.BlockSpec((tm, 2 * d_kv), lambda i: (i, 0)),
        ],
        out_shape=[
            jax.ShapeDtypeStruct((n, d_q), BF16),
            jax.ShapeDtypeStruct((n, 2 * d_kv), BF16),
            jax.ShapeDtypeStruct((n, 2 * d_kv), BF16),
        ],
        compiler_params=_params("arbitrary"),
        name="qkv_norm_rope",
    )(hn, positions.reshape(n, 1), w_qkv, e, et, _row(gain), _row(invf))


def _attn_kernel(sinks_ref, q_ref, kc_ref, kp_ref, vc_ref, vp_ref, o_ref):
    blk = q_ref.shape[0]
    n = pl.program_id(1)
    lo = lax.broadcasted_iota(jnp.int32, (1, LANES), 1) < HEAD_DIM
    rows = GROUP_SIZE * blk
    qi = lax.broadcasted_iota(jnp.int32, (rows, 2 * blk), 0) % blk
    kj = lax.broadcasted_iota(jnp.int32, (rows, 2 * blk), 1)
    valid = (kj > qi) & (kj <= qi + blk) & ((kj >= blk) | (n > 0))
    bias = jnp.where(valid, 0.0, NEG_BIG)
    contract_last = (((1,), (1,)), ((), ()))
    for g in range(N_KV_HEADS):
        gcols = slice(g * LANES, (g + 1) * LANES)
        kband = jnp.concatenate([kp_ref[:, gcols], kc_ref[:, gcols]], axis=0)
        vband = jnp.concatenate([vp_ref[:, gcols], vc_ref[:, gcols]], axis=0)
        parts, sink_parts = [], []
        for h in range(GROUP_SIZE):
            pair = g * (GROUP_SIZE // 2) + h // 2
            qp = q_ref[:, pair * LANES:(pair + 1) * LANES]
            keep = lo if h % 2 == 0 else jnp.logical_not(lo)
            parts.append(jnp.where(keep, qp, jnp.zeros_like(qp)))
            sink_parts.append(jnp.full((blk, 1), sinks_ref[g * GROUP_SIZE + h], F32))
        qs = jnp.concatenate(parts, axis=0)
        sink = jnp.concatenate(sink_parts, axis=0)
        s = lax.dot_general(qs, kband, contract_last, preferred_element_type=F32) + bias
        m = jnp.maximum(jnp.max(s, axis=-1, keepdims=True), sink)
        p = jnp.exp(s - m)
        denom = jnp.sum(p, axis=-1, keepdims=True) + jnp.exp(sink - m)
        o = _dot(p.astype(BF16), vband) * (1.0 / denom)
        for h2 in range(GROUP_SIZE // 2):
            even = o[(2 * h2) * blk:(2 * h2 + 1) * blk]
            odd = o[(2 * h2 + 1) * blk:(2 * h2 + 2) * blk]
            pair = g * (GROUP_SIZE // 2) + h2
            o_ref[:, pair * LANES:(pair + 1) * LANES] = jnp.where(lo, even, odd).astype(BF16)


def _attention(q, kd, vd, sinks, batch, seq):
    n, d_q = q.shape
    blk = ATT_BLOCK
    nb = seq // blk
    w_kv = kd.shape[1]
    cur = lambda b, i: (b * nb + i, 0)
    prev = lambda b, i: (b * nb + jnp.maximum(i - 1, 0), 0)
    return pl.pallas_call(
        _attn_kernel,
        grid=(batch, nb),
        in_specs=[
            pl.BlockSpec(memory_space=pltpu.SMEM),
            pl.BlockSpec((blk, d_q), cur),
            pl.BlockSpec((blk, w_kv), cur),
            pl.BlockSpec((blk, w_kv), prev),
            pl.BlockSpec((blk, w_kv), cur),
            pl.BlockSpec((blk, w_kv), prev),
        ],
        out_specs=pl.BlockSpec((blk, d_q), cur),
        out_shape=jax.ShapeDtypeStruct((n, d_q), BF16),
        compiler_params=_params("arbitrary", "arbitrary"),
        name="banded_sink_attention",
    )(sinks.astype(F32), q, kd, kd, vd, vd)


def _oproj_kernel(a_ref, x_ref, w_ref, gn_ref, xo_ref, hn_ref):
    out = x_ref[...] + _dot(a_ref[...], w_ref[...])
    xo_ref[...] = out
    hn_ref[...] = _rms_scale(out, gn_ref[...]).astype(BF16)


def _oproj(a, x, w, g_next, tm=512):
    n, d = x.shape
    k = a.shape[1]
    return pl.pallas_call(
        _oproj_kernel,
        grid=(n // tm,),
        in_specs=[
            pl.BlockSpec((tm, k), lambda i: (i, 0)),
            pl.BlockSpec((tm, d), lambda i: (i, 0)),
            pl.BlockSpec((k, d), lambda i: (0, 0)),
            pl.BlockSpec((1, d), lambda i: (0, 0)),
        ],
        out_specs=[pl.BlockSpec((tm, d), lambda i: (i, 0)), pl.BlockSpec((tm, d), lambda i: (i, 0))],
        out_shape=[jax.ShapeDtypeStruct((n, d), F32), jax.ShapeDtypeStruct((n, d), BF16)],
        compiler_params=_params("arbitrary"),
        name="attn_out_proj",
    )(a, x, w, _row(g_next))


def kernel(x, positions, l0_norm_g, l0_a_w_in, l0_a_b_in, l0_a_dw_w, l0_a_dw_b, l0_a_ln_g, l0_a_ln_b, l0_a_w_out, l0_a_b_out, l0_ffn_norm_g, l0_ffn_w_up, l0_ffn_dw_w, l0_ffn_dw_b, l0_ffn_w_down, l1_norm_g, l1_b_w_group, l1_b_scale, l1_ffn_norm_g, l1_ffn_w_up, l1_ffn_dw_w, l1_ffn_dw_b, l1_ffn_w_down, l2_norm_g, l2_c_w_qkv, l2_c_q_norm_g, l2_c_k_norm_g, l2_c_sinks, l2_c_w_o, l2_ffn_norm_g, l2_ffn_w_up, l2_ffn_dw_w, l2_ffn_dw_b, l2_ffn_w_down, l3_norm_g, l3_a_w_in, l3_a_b_in, l3_a_dw_w, l3_a_dw_b, l3_a_ln_g, l3_a_ln_b, l3_a_w_out, l3_a_b_out, l3_ffn_norm_g, l3_ffn_w_up, l3_ffn_dw_w, l3_ffn_dw_b, l3_ffn_w_down):
    batch, seq, d = x.shape
    n = batch * seq
    xs = x.reshape(n, d)
    bf = lambda w: w.astype(BF16)

    def conformer(xs, hn, w_in, b_in, dw_w, dw_b, ln_g, ln_b, w_out, b_out, g_next):
        glu = _conformer_glu(hn, bf(w_in), b_in)
        return _conformer_out(glu, xs, dw_w, dw_b, ln_g, ln_b, bf(w_out), b_out, g_next, seq)

    def ffn(xs, hn, w_up, dw_w, dw_b, w_down, g_next):
        return _ffn(hn, xs, bf(w_up), dw_w, dw_b, bf(w_down), g_next, seq)

    hn = _prenorm(xs, l0_norm_g)
    xs, hn = conformer(xs, hn, l0_a_w_in, l0_a_b_in, l0_a_dw_w, l0_a_dw_b, l0_a_ln_g, l0_a_ln_b, l0_a_w_out,
                       l0_a_b_out, l0_ffn_norm_g)
    xs, hn = ffn(xs, hn, l0_ffn_w_up, l0_ffn_dw_w, l0_ffn_dw_b, l0_ffn_w_down, l1_norm_g)
    xs, hn = _pool_mixer(hn, xs, bf(l1_b_w_group), l1_b_scale, l1_ffn_norm_g, seq)
    xs, hn = ffn(xs, hn, l1_ffn_w_up, l1_ffn_dw_w, l1_ffn_dw_b, l1_ffn_w_down, l2_norm_g)
    q, kd, vd = _qkv(hn, positions, bf(l2_c_w_qkv), l2_c_q_norm_g, l2_c_k_norm_g)
    att = _attention(q, kd, vd, l2_c_sinks, batch, seq)
    xs, hn = _oproj(att, xs, bf(l2_c_w_o), l2_ffn_norm_g)
    xs, hn = ffn(xs, hn, l2_ffn_w_up, l2_ffn_dw_w, l2_ffn_dw_b, l2_ffn_w_down, l3_norm_g)
    xs, hn = conformer(xs, hn, l3_a_w_in, l3_a_b_in, l3_a_dw_w, l3_a_dw_b, l3_a_ln_g, l3_a_ln_b, l3_a_w_out,
                       l3_a_b_out, l3_ffn_norm_g)
    xs, hn = ffn(xs, hn, l3_ffn_w_up, l3_ffn_dw_w, l3_ffn_dw_b, l3_ffn_w_down, l3_ffn_norm_g)
    return xs.reshape(batch, seq, d)
```

```python
import functools

import jax
import jax.numpy as jnp
from jax import lax
from jax.experimental import pallas as pl
from jax.experimental.pallas import tpu as pltpu

F32 = jnp.float32
BF16 = jnp.bfloat16

EPS = 1e-6
LANES = 128
HEAD_DIM = 64
N_HEADS = 32
N_KV_HEADS = 4
GROUP_SIZE = N_HEADS // N_KV_HEADS
ATT_BLOCK = 128
ROT_DIM = 16
ROPE_THETA = 500000.0
POOL_WINDOWS = (2, 4, 8, 16)
CONF_WIDTH = 31
FFN_CONV_WIDTH = 3
CONF_HALO = 32
POOL_HALO = 16
FFN_HALO = 8
NEG_BIG = -1e30

VMEM_LIMIT_BYTES = 56 * 1024 * 1024


def _params(*semantics):
    return pltpu.CompilerParams(dimension_semantics=semantics, vmem_limit_bytes=VMEM_LIMIT_BYTES)


def _sigmoid(x):
    return 1.0 / (1.0 + jnp.exp(-x))


def _rms_scale(x, g):
    ms = jnp.mean(x * x, axis=-1, keepdims=True)
    return x * lax.rsqrt(ms + EPS) * g


def _dot(a, b):
    return jnp.dot(a, b, preferred_element_type=F32)


def _row(v):
    return v.reshape(1, -1)


def _prenorm_kernel(x_ref, g_ref, hn_ref):
    hn_ref[...] = _rms_scale(x_ref[...], g_ref[...]).astype(BF16)


def _prenorm(x, g, tm=1024):
    n, d = x.shape
    return pl.pallas_call(
        _prenorm_kernel,
        grid=(n // tm,),
        in_specs=[pl.BlockSpec((tm, d), lambda i: (i, 0)), pl.BlockSpec((1, d), lambda i: (0, 0))],
        out_specs=pl.BlockSpec((tm, d), lambda i: (i, 0)),
        out_shape=jax.ShapeDtypeStruct((n, d), BF16),
        compiler_params=_params("arbitrary"),
        name="prenorm",
    )(x, _row(g))


def _glu_kernel(hn_ref, wa_ref, wg_ref, ba_ref, bg_ref, o_ref):
    hn = hn_ref[...]
    a = _dot(hn, wa_ref[...]) + ba_ref[...]
    g = _dot(hn, wg_ref[...]) + bg_ref[...]
    o_ref[...] = (a * _sigmoid(g)).astype(BF16)


def _conformer_glu(hn, w_in, b_in, tm=1024, tn=1024):
    n, d = hn.shape
    nb = d // tn
    b_in = _row(b_in)
    return pl.pallas_call(
        _glu_kernel,
        grid=(n // tm, nb),
        in_specs=[
            pl.BlockSpec((tm, d), lambda i, j: (i, 0)),
            pl.BlockSpec((d, tn), lambda i, j: (0, j)),
            pl.BlockSpec((d, tn), lambda i, j: (0, j + nb)),
            pl.BlockSpec((1, tn), lambda i, j: (0, j)),
            pl.BlockSpec((1, tn), lambda i, j: (0, j + nb)),
        ],
        out_specs=pl.BlockSpec((tm, tn), lambda i, j: (i, j)),
        out_shape=jax.ShapeDtypeStruct((n, d), BF16),
        compiler_params=_params("arbitrary", "arbitrary"),
        name="conformer_glu",
    )(hn, w_in, w_in, b_in, b_in)


def _conformer_out_kernel(glu_ref, halo_ref, x_ref, dww_ref, dwb_ref, lng_ref, lnb_ref, wo_ref, bo_ref, gn_ref,
                          xo_ref, hn_ref, s_ref, y_ref, *, tiles_per_seq, row_chunk):
    tm, d = glu_ref.shape
    ncol = d // LANES
    first = (pl.program_id(0) % tiles_per_seq) == 0
    halo = jnp.where(first, 0.0, halo_ref[...].astype(F32))
    cur = glu_ref[...].astype(F32)
    for c in range(ncol):
        s_ref[c, 0:CONF_HALO, :] = halo[:, c * LANES:(c + 1) * LANES]
        s_ref[c, CONF_HALO:CONF_HALO + tm, :] = cur[:, c * LANES:(c + 1) * LANES]

    base = CONF_HALO - (CONF_WIDTH - 1)

    def col_body(c, carry):
        bias = dwb_ref[c]
        for r in range(tm // row_chunk):
            acc = jnp.broadcast_to(bias, (row_chunk, LANES))
            for j in range(CONF_WIDTH):
                acc = acc + dww_ref[c, j:j + 1, :] * s_ref[c, pl.ds(r * row_chunk + base + j, row_chunk), :]
            y_ref[c, r * row_chunk:(r + 1) * row_chunk, :] = acc
        return carry

    lax.fori_loop(0, ncol, col_body, 0)

    y = jnp.concatenate([y_ref[c] for c in range(ncol)], axis=1)
    mu = jnp.mean(y, axis=-1, keepdims=True)
    yc = y - mu
    var = jnp.mean(yc * yc, axis=-1, keepdims=True)
    z = yc * lax.rsqrt(var + EPS) * lng_ref[...] + lnb_ref[...]
    a = (z * _sigmoid(z)).astype(BF16)
    out = x_ref[...] + (_dot(a, wo_ref[...]) + bo_ref[...])
    xo_ref[...] = out
    hn_ref[...] = _rms_scale(out, gn_ref[...]).astype(BF16)


def _conformer_out(glu, x, dw_w, dw_b, ln_g, ln_b, w_out, b_out, g_next, seq, tm=512, row_chunk=64):
    n, d = x.shape
    ncol = d // LANES
    halo_blocks = tm // CONF_HALO
    dww = dw_w.reshape(CONF_WIDTH, ncol, LANES).transpose(1, 0, 2)
    dwb = dw_b.reshape(ncol, 1, LANES)
    kern = functools.partial(_conformer_out_kernel, tiles_per_seq=seq // tm, row_chunk=row_chunk)
    vec = pl.BlockSpec((1, d), lambda i: (0, 0))
    return pl.pallas_call(
        kern,
        grid=(n // tm,),
        in_specs=[
            pl.BlockSpec((tm, d), lambda i: (i, 0)),
            pl.BlockSpec((CONF_HALO, d), lambda i: (jnp.maximum(i * halo_blocks - 1, 0), 0)),
            pl.BlockSpec((tm, d), lambda i: (i, 0)),
            pl.BlockSpec((ncol, CONF_WIDTH, LANES), lambda i: (0, 0, 0)),
            pl.BlockSpec((ncol, 1, LANES), lambda i: (0, 0, 0)),
            vec, vec,
            pl.BlockSpec((d, d), lambda i: (0, 0)),
            vec, vec,
        ],
        out_specs=[pl.BlockSpec((tm, d), lambda i: (i, 0)), pl.BlockSpec((tm, d), lambda i: (i, 0))],
        out_shape=[jax.ShapeDtypeStruct((n, d), F32), jax.ShapeDtypeStruct((n, d), BF16)],
        scratch_shapes=[pltpu.VMEM((ncol, tm + CONF_HALO, LANES), F32), pltpu.VMEM((ncol, tm, LANES), F32)],
        compiler_params=_params("arbitrary"),
        name="conformer_out",
    )(glu, glu, x, dww, dwb, _row(ln_g), _row(ln_b), w_out, _row(b_out), _row(g_next))


def _ffn_kernel(hn_ref, x_ref, wg_ref, wv_ref, cwg_ref, cwv_ref, cbg_ref, cbv_ref, wd_ref, gn_ref, *rest,
                tiles_per_seq, emit_hn):
    if emit_hn:
        xo_ref, hn_out_ref, sg_ref, sv_ref, carry_g_ref, carry_v_ref = rest
    else:
        xo_ref, sg_ref, sv_ref, carry_g_ref, carry_v_ref = rest
    tm = hn_ref.shape[0]
    j = pl.program_id(1)
    nj = pl.num_programs(1)
    first = (pl.program_id(0) % tiles_per_seq) == 0

    @pl.when(j == 0)
    def _():
        xo_ref[...] = x_ref[...]

    hn = hn_ref[...]

    def conv_branch(w_ref, s_ref, carry_ref, cw_ref, cb_ref):
        u = _dot(hn, w_ref[...])
        s_ref[0:FFN_HALO, :] = jnp.where(first, 0.0, carry_ref[j])
        s_ref[FFN_HALO:FFN_HALO + tm, :] = u
        carry_ref[j] = u[tm - FFN_HALO:, :]
        out = cb_ref[...] + cw_ref[FFN_CONV_WIDTH - 1:FFN_CONV_WIDTH, :] * u
        for k in range(1, FFN_CONV_WIDTH):
            out = out + cw_ref[FFN_CONV_WIDTH - 1 - k:FFN_CONV_WIDTH - k, :] * s_ref[pl.ds(FFN_HALO - k, tm), :]
        return out

    gate = conv_branch(wg_ref, sg_ref, carry_g_ref, cwg_ref, cbg_ref)
    val = conv_branch(wv_ref, sv_ref, carry_v_ref, cwv_ref, cbv_ref)
    p = (gate * _sigmoid(gate) * val).astype(BF16)
    xo_ref[...] += _dot(p, wd_ref[...])

    if emit_hn:
        @pl.when(j == nj - 1)
        def _():
            hn_out_ref[...] = _rms_scale(xo_ref[...], gn_ref[...]).astype(BF16)


def _ffn(hn, x, w_up, dw_w, dw_b, w_down, g_next, seq, tm=512, tf=512):
    n, d = x.shape
    dff = w_down.shape[0]
    nj = dff // tf
    dw_b = _row(dw_b)
    emit_hn = g_next is not None
    if not emit_hn:
        g_next = jnp.ones((d,), F32)
    kern = functools.partial(_ffn_kernel, tiles_per_seq=seq // tm, emit_hn=emit_hn)
    row_tile = pl.BlockSpec((tm, d), lambda i, j: (i, 0))
    n_out = 2 if emit_hn else 1
    res = pl.pallas_call(
        kern,
        grid=(n // tm, nj),
        in_specs=[
            row_tile,
            row_tile,
            pl.BlockSpec((d, tf), lambda i, j: (0, j)),
            pl.BlockSpec((d, tf), lambda i, j: (0, j + nj)),
            pl.BlockSpec((FFN_CONV_WIDTH, tf), lambda i, j: (0, j)),
            pl.BlockSpec((FFN_CONV_WIDTH, tf), lambda i, j: (0, j + nj)),
            pl.BlockSpec((1, tf), lambda i, j: (0, j)),
            pl.BlockSpec((1, tf), lambda i, j: (0, j + nj)),
            pl.BlockSpec((tf, d), lambda i, j: (j, 0)),
            pl.BlockSpec((1, d), lambda i, j: (0, 0)),
        ],
        out_specs=[row_tile, row_tile][:n_out],
        out_shape=[jax.ShapeDtypeStruct((n, d), F32), jax.ShapeDtypeStruct((n, d), BF16)][:n_out],
        scratch_shapes=[
            pltpu.VMEM((tm + FFN_HALO, tf), F32),
            pltpu.VMEM((tm + FFN_HALO, tf), F32),
            pltpu.VMEM((nj, FFN_HALO, tf), F32),
            pltpu.VMEM((nj, FFN_HALO, tf), F32),
        ],
        compiler_params=_params("arbitrary", "arbitrary"),
        name="conv_gated_mlp",
    )(hn, x, w_up, w_up, dw_w, dw_w, dw_b, dw_b, w_down, _row(g_next))
    return (res[0], res[1]) if emit_hn else (res[0], None)


def _pool_kernel(hn_ref, halo_ref, x_ref, wgrp_ref, scale_ref, gn_ref, xo_ref, hn_out_ref, s_ref, m_ref,
                 *, tiles_per_seq, row_chunk):
    tm, d = hn_ref.shape
    gdim = d // len(POOL_WINDOWS)
    tile = pl.program_id(0) % tiles_per_seq
    first = tile == 0
    s_ref[0:POOL_HALO, :] = jnp.where(first, 0.0, halo_ref[...].astype(F32))
    s_ref[POOL_HALO:POOL_HALO + tm, :] = hn_ref[...].astype(F32)

    t = tile * tm + lax.broadcasted_iota(jnp.int32, (tm, 1), 0)
    for g, w in enumerate(POOL_WINDOWS):
        cols = slice(g * gdim, (g + 1) * gdim)
        inv_count = 1.0 / jnp.minimum(t + 1, w).astype(F32)
        for r in range(tm // row_chunk):
            rows = r * row_chunk
            cur = s_ref[POOL_HALO + rows:POOL_HALO + rows + row_chunk, cols]
            acc = cur
            for k in range(1, w):
                acc = acc + s_ref[pl.ds(POOL_HALO + rows - k, row_chunk), cols]
            mixed = acc * inv_count[rows:rows + row_chunk] - cur
            m_ref[rows:rows + row_chunk, cols] = mixed.astype(BF16)

    y = jnp.concatenate(
        [_dot(m_ref[:, g * gdim:(g + 1) * gdim], wgrp_ref[g]) for g in range(len(POOL_WINDOWS))], axis=1)
    out = x_ref[...] + y * scale_ref[...]
    xo_ref[...] = out
    hn_out_ref[...] = _rms_scale(out, gn_ref[...]).astype(BF16)


def _pool_mixer(hn, x, w_group, scale, g_next, seq, tm=512, row_chunk=64):
    n, d = x.shape
    ng, gdim, _ = w_group.shape
    halo_blocks = tm // POOL_HALO
    kern = functools.partial(_pool_kernel, tiles_per_seq=seq // tm, row_chunk=row_chunk)
    vec = pl.BlockSpec((1, d), lambda i: (0, 0))
    return pl.pallas_call(
        kern,
        grid=(n // tm,),
        in_specs=[
            pl.BlockSpec((tm, d), lambda i: (i, 0)),
            pl.BlockSpec((POOL_HALO, d), lambda i: (jnp.maximum(i * halo_blocks - 1, 0), 0)),
            pl.BlockSpec((tm, d), lambda i: (i, 0)),
            pl.BlockSpec((ng, gdim, gdim), lambda i: (0, 0, 0)),
            vec, vec,
        ],
        out_specs=[pl.BlockSpec((tm, d), lambda i: (i, 0)), pl.BlockSpec((tm, d), lambda i: (i, 0))],
        out_shape=[jax.ShapeDtypeStruct((n, d), F32), jax.ShapeDtypeStruct((n, d), BF16)],
        scratch_shapes=[pltpu.VMEM((tm + POOL_HALO, d), F32), pltpu.VMEM((tm, d), BF16)],
        compiler_params=_params("arbitrary"),
        name="pool_mixer",
    )(hn, hn, x, w_group, _row(scale), _row(g_next))


def _qkv_kernel(hn_ref, pos_ref, w_ref, e_ref, et_ref, gain_ref, invf_ref, q_ref, k_ref, v_ref):
    d_q = q_ref.shape[1]
    d_kv = N_KV_HEADS * HEAD_DIM
    d_qk = d_q + d_kv
    acc = _dot(hn_ref[...], w_ref[...])
    qk = acc[:, :d_qk]

    ms = _dot((qk * qk).astype(BF16), e_ref[...]) * (1.0 / HEAD_DIM)
    scale = lax.rsqrt(ms + EPS)
    scale_hi = scale.astype(BF16)
    scale_lo = (scale - scale_hi.astype(F32)).astype(BF16)
    scale_b = _dot(scale_hi, et_ref[...]) + _dot(scale_lo, et_ref[...])
    qkn = qk * scale_b * gain_ref[...]

    half = ROT_DIM // 2
    lane = lax.broadcasted_iota(jnp.int32, (1, LANES), 1) % HEAD_DIM
    ang = pos_ref[...].astype(F32) * invf_ref[...]
    cos = jnp.cos(ang)
    sin = jnp.sin(ang)
    c_self = jnp.where(lane < ROT_DIM, cos, 1.0)
    c_up = jnp.where(lane < half, -sin, 0.0)
    c_dn = jnp.where((lane >= half) & (lane < ROT_DIM), sin, 0.0)
    cols = []
    for c in range(d_qk // LANES):
        xc = qkn[:, c * LANES:(c + 1) * LANES]
        cols.append(xc * c_self + pltpu.roll(xc, LANES - half, 1) * c_up + pltpu.roll(xc, half, 1) * c_dn)

    nq = d_q // LANES
    sm_scale = 1.0 / (HEAD_DIM ** 0.5)
    q_ref[...] = (jnp.concatenate(cols[:nq], axis=1) * sm_scale).astype(BF16)

    lo = lax.broadcasted_iota(jnp.int32, (1, LANES), 1) < HEAD_DIM

    def dup(xc):
        r = pltpu.roll(xc, HEAD_DIM, 1)
        return [jnp.where(lo, xc, r), jnp.where(lo, r, xc)]

    kd, vd = [], []
    for c in range(d_kv // LANES):
        kd += dup(cols[nq + c])
        vd += dup(acc[:, d_qk + c * LANES:d_qk + (c + 1) * LANES])
    k_ref[...] = jnp.concatenate(kd, axis=1).astype(BF16)
    v_ref[...] = jnp.concatenate(vd, axis=1).astype(BF16)


def _qkv(hn, positions, w_qkv, q_gain, k_gain, tm=512):
    n, d = hn.shape
    d_q = N_HEADS * HEAD_DIM
    d_kv = N_KV_HEADS * HEAD_DIM
    d_qk = d_q + d_kv
    d_all = d_q + 2 * d_kv
    head_of_col = jnp.arange(d_qk) // HEAD_DIM
    e = (head_of_col[:, None] == jnp.arange(LANES)[None, :]).astype(BF16)
    et = e.T
    gain = jnp.concatenate([jnp.tile(q_gain, N_HEADS), jnp.tile(k_gain, N_KV_HEADS)]).astype(F32)
    inv_freq = ROPE_THETA ** (-jnp.arange(0, ROT_DIM, 2, dtype=F32) / ROT_DIM)
    lane = jnp.arange(LANES) % HEAD_DIM
    invf = jnp.where(lane < ROT_DIM, inv_freq[lane % (ROT_DIM // 2)], 0.0).astype(F32)
    const = lambda shape: pl.BlockSpec(shape, lambda i: (0, 0))
    return pl.pallas_call(
        _qkv_kernel,
        grid=(n // tm,),
        in_specs=[
            pl.BlockSpec((tm, d), lambda i: (i, 0)),
            pl.BlockSpec((tm, 1), lambda i: (i, 0)),
            const((d, d_all)), const((d_qk, LANES)), const((LANES, d_qk)), const((1, d_qk)), const((1, LANES)),
        ],
        out_specs=[
            pl.BlockSpec((tm, d_q), lambda i: (i, 0)),
            pl.BlockSpec((tm, 2 * d_kv), lambda i: (i, 0)),
            pl.BlockSpec((tm, 2 * d_kv), lambda i: (i, 0)),
        ],
        out_shape=[
            jax.ShapeDtypeStruct((n, d_q), BF16),
            jax.ShapeDtypeStruct((n, 2 * d_kv), BF16),
            jax.ShapeDtypeStruct((n, 2 * d_kv), BF16),
        ],
        compiler_params=_params("arbitrary"),
        name="qkv_norm_rope",
    )(hn, positions.reshape(n, 1), w_qkv, e, et, _row(gain), _row(invf))


def _attn_kernel(sinks_ref, q_ref, kc_ref, kp_ref, vc_ref, vp_ref, o_ref):
    blk = q_ref.shape[0]
    n = pl.program_id(1)
    lo = lax.broadcasted_iota(jnp.int32, (1, LANES), 1) < HEAD_DIM
    rows = GROUP_SIZE * blk
    qi = lax.broadcasted_iota(jnp.int32, (rows, 2 * blk), 0) % blk
    kj = lax.broadcasted_iota(jnp.int32, (rows, 2 * blk), 1)
    valid = (kj > qi) & (kj <= qi + blk) & ((kj >= blk) | (n > 0))
    bias = jnp.where(valid, 0.0, NEG_BIG)
    contract_last = (((1,), (1,)), ((), ()))
    for g in range(N_KV_HEADS):
        gcols = slice(g * LANES, (g + 1) * LANES)
        kband = jnp.concatenate([kp_ref[:, gcols], kc_ref[:, gcols]], axis=0)
        vband = jnp.concatenate([vp_ref[:, gcols], vc_ref[:, gcols]], axis=0)
        parts, sink_parts = [], []
        for h in range(GROUP_SIZE):
            pair = g * (GROUP_SIZE // 2) + h // 2
            qp = q_ref[:, pair * LANES:(pair + 1) * LANES]
            keep = lo if h % 2 == 0 else jnp.logical_not(lo)
            parts.append(jnp.where(keep, qp, jnp.zeros_like(qp)))
            sink_parts.append(jnp.full((blk, 1), sinks_ref[g * GROUP_SIZE + h], F32))
        qs = jnp.concatenate(parts, axis=0)
        sink = jnp.concatenate(sink_parts, axis=0)
        s = lax.dot_general(qs, kband, contract_last, preferred_element_type=F32) + bias
        m = jnp.maximum(jnp.max(s, axis=-1, keepdims=True), sink)
        p = jnp.exp(s - m)
        denom = jnp.sum(p, axis=-1, keepdims=True) + jnp.exp(sink - m)
        o = _dot(p.astype(BF16), vband) * (1.0 / denom)
        for h2 in range(GROUP_SIZE // 2):
            even = o[(2 * h2) * blk:(2 * h2 + 1) * blk]
            odd = o[(2 * h2 + 1) * blk:(2 * h2 + 2) * blk]
            pair = g * (GROUP_SIZE // 2) + h2
            o_ref[:, pair * LANES:(pair + 1) * LANES] = jnp.where(lo, even, odd).astype(BF16)


def _attention(q, kd, vd, sinks, batch, seq):
    n, d_q = q.shape
    blk = ATT_BLOCK
    nb = seq // blk
    w_kv = kd.shape[1]
    cur = lambda b, i: (b * nb + i, 0)
    prev = lambda b, i: (b * nb + jnp.maximum(i - 1, 0), 0)
    return pl.pallas_call(
        _attn_kernel,
        grid=(batch, nb),
        in_specs=[
            pl.BlockSpec(memory_space=pltpu.SMEM),
            pl.BlockSpec((blk, d_q), cur),
            pl.BlockSpec((blk, w_kv), cur),
            pl.BlockSpec((blk, w_kv), prev),
            pl.BlockSpec((blk, w_kv), cur),
            pl.BlockSpec((blk, w_kv), prev),
        ],
        out_specs=pl.BlockSpec((blk, d_q), cur),
        out_shape=jax.ShapeDtypeStruct((n, d_q), BF16),
        compiler_params=_params("arbitrary", "arbitrary"),
        name="banded_sink_attention",
    )(sinks.astype(F32), q, kd, kd, vd, vd)


def _oproj_kernel(a_ref, x_ref, w_ref, gn_ref, xo_ref, hn_ref):
    out = x_ref[...] + _dot(a_ref[...], w_ref[...])
    xo_ref[...] = out
    hn_ref[...] = _rms_scale(out, gn_ref[...]).astype(BF16)


def _oproj(a, x, w, g_next, tm=512):
    n, d = x.shape
    k = a.shape[1]
    return pl.pallas_call(
        _oproj_kernel,
        grid=(n // tm,),
        in_specs=[
            pl.BlockSpec((tm, k), lambda i: (i, 0)),
            pl.BlockSpec((tm, d), lambda i: (i, 0)),
            pl.BlockSpec((k, d), lambda i: (0, 0)),
            pl.BlockSpec((1, d), lambda i: (0, 0)),
        ],
        out_specs=[pl.BlockSpec((tm, d), lambda i: (i, 0)), pl.BlockSpec((tm, d), lambda i: (i, 0))],
        out_shape=[jax.ShapeDtypeStruct((n, d), F32), jax.ShapeDtypeStruct((n, d), BF16)],
        compiler_params=_params("arbitrary"),
        name="attn_out_proj",
    )(a, x, w, _row(g_next))


def kernel(x, positions, l0_norm_g, l0_a_w_in, l0_a_b_in, l0_a_dw_w, l0_a_dw_b, l0_a_ln_g, l0_a_ln_b, l0_a_w_out, l0_a_b_out, l0_ffn_norm_g, l0_ffn_w_up, l0_ffn_dw_w, l0_ffn_dw_b, l0_ffn_w_down, l1_norm_g, l1_b_w_group, l1_b_scale, l1_ffn_norm_g, l1_ffn_w_up, l1_ffn_dw_w, l1_ffn_dw_b, l1_ffn_w_down, l2_norm_g, l2_c_w_qkv, l2_c_q_norm_g, l2_c_k_norm_g, l2_c_sinks, l2_c_w_o, l2_ffn_norm_g, l2_ffn_w_up, l2_ffn_dw_w, l2_ffn_dw_b, l2_ffn_w_down, l3_norm_g, l3_a_w_in, l3_a_b_in, l3_a_dw_w, l3_a_dw_b, l3_a_ln_g, l3_a_ln_b, l3_a_w_out, l3_a_b_out, l3_ffn_norm_g, l3_ffn_w_up, l3_ffn_dw_w, l3_ffn_dw_b, l3_ffn_w_down):
    batch, seq, d = x.shape
    n = batch * seq
    xs = x.reshape(n, d)
    bf = lambda w: w.astype(BF16)

    def conformer(xs, hn, w_in, b_in, dw_w, dw_b, ln_g, ln_b, w_out, b_out, g_next):
        glu = _conformer_glu(hn, bf(w_in), b_in)
        return _conformer_out(glu, xs, dw_w, dw_b, ln_g, ln_b, bf(w_out), b_out, g_next, seq)

    def ffn(xs, hn, w_up, dw_w, dw_b, w_down, g_next):
        return _ffn(hn, xs, bf(w_up), dw_w, dw_b, bf(w_down), g_next, seq)

    hn = _prenorm(xs, l0_norm_g)
    xs, hn = conformer(xs, hn, l0_a_w_in, l0_a_b_in, l0_a_dw_w, l0_a_dw_b, l0_a_ln_g, l0_a_ln_b, l0_a_w_out,
                       l0_a_b_out, l0_ffn_norm_g)
    xs, hn = ffn(xs, hn, l0_ffn_w_up, l0_ffn_dw_w, l0_ffn_dw_b, l0_ffn_w_down, l1_norm_g)
    xs, hn = _pool_mixer(hn, xs, bf(l1_b_w_group), l1_b_scale, l1_ffn_norm_g, seq)
    xs, hn = ffn(xs, hn, l1_ffn_w_up, l1_ffn_dw_w, l1_ffn_dw_b, l1_ffn_w_down, l2_norm_g)
    q, kd, vd = _qkv(hn, positions, bf(l2_c_w_qkv), l2_c_q_norm_g, l2_c_k_norm_g)
    att = _attention(q, kd, vd, l2_c_sinks, batch, seq)
    xs, hn = _oproj(att, xs, bf(l2_c_w_o), l2_ffn_norm_g)
    xs, hn = ffn(xs, hn, l2_ffn_w_up, l2_ffn_dw_w, l2_ffn_dw_b, l2_ffn_w_down, l3_norm_g)
    xs, hn = conformer(xs, hn, l3_a_w_in, l3_a_b_in, l3_a_dw_w, l3_a_dw_b, l3_a_ln_g, l3_a_ln_b, l3_a_w_out,
                       l3_a_b_out, l3_ffn_norm_g)
    xs, _ = ffn(xs, hn, l3_ffn_w_up, l3_ffn_dw_w, l3_ffn_dw_b, l3_ffn_w_down, None)
    return xs.reshape(batch, seq, d)
```

```python
import functools

import jax
import jax.numpy as jnp
from jax import lax
from jax.experimental import pallas as pl
from jax.experimental.pallas import tpu as pltpu

F32 = jnp.float32
BF16 = jnp.bfloat16

EPS = 1e-6
LANES = 128
HEAD_DIM = 64
N_HEADS = 32
N_KV_HEADS = 4
GROUP_SIZE = N_HEADS // N_KV_HEADS
ATT_BLOCK = 128
ROT_DIM = 16
ROPE_THETA = 500000.0
POOL_WINDOWS = (2, 4, 8, 16)
CONF_WIDTH = 31
FFN_CONV_WIDTH = 3
CONF_HALO = 32
POOL_HALO = 16
FFN_HALO = 8
NEG_BIG = -1e30

VMEM_LIMIT_BYTES = 56 * 1024 * 1024


def _params(*semantics):
    return pltpu.CompilerParams(dimension_semantics=semantics, vmem_limit_bytes=VMEM_LIMIT_BYTES)


def _sigmoid(x):
    return 1.0 / (1.0 + jnp.exp(-x))


def _rms_scale(x, g):
    ms = jnp.mean(x * x, axis=-1, keepdims=True)
    return x * lax.rsqrt(ms + EPS) * g


def _dot(a, b):
    return jnp.dot(a, b, preferred_element_type=F32)


def _row(v):
    return v.reshape(1, -1)


def _prenorm_kernel(x_ref, g_ref, hn_ref):
    hn_ref[...] = _rms_scale(x_ref[...], g_ref[...]).astype(BF16)


def _prenorm(x, g, tm=1024):
    n, d = x.shape
    return pl.pallas_call(
        _prenorm_kernel,
        grid=(n // tm,),
        in_specs=[pl.BlockSpec((tm, d), lambda i: (i, 0)), pl.BlockSpec((1, d), lambda i: (0, 0))],
        out_specs=pl.BlockSpec((tm, d), lambda i: (i, 0)),
        out_shape=jax.ShapeDtypeStruct((n, d), BF16),
        compiler_params=_params("arbitrary"),
        name="prenorm",
    )(x, _row(g))


def _glu_kernel(hn_ref, wa_ref, wg_ref, ba_ref, bg_ref, o_ref):
    hn = hn_ref[...]
    a = _dot(hn, wa_ref[...]) + ba_ref[...]
    g = _dot(hn, wg_ref[...]) + bg_ref[...]
    o_ref[...] = (a * _sigmoid(g)).astype(BF16)


def _conformer_glu(hn, w_in, b_in, tm=1024, tn=1024):
    n, d = hn.shape
    nb = d // tn
    b_in = _row(b_in)
    return pl.pallas_call(
        _glu_kernel,
        grid=(n // tm, nb),
        in_specs=[
            pl.BlockSpec((tm, d), lambda i, j: (i, 0)),
            pl.BlockSpec((d, tn), lambda i, j: (0, j)),
            pl.BlockSpec((d, tn), lambda i, j: (0, j + nb)),
            pl.BlockSpec((1, tn), lambda i, j: (0, j)),
            pl.BlockSpec((1, tn), lambda i, j: (0, j + nb)),
        ],
        out_specs=pl.BlockSpec((tm, tn), lambda i, j: (i, j)),
        out_shape=jax.ShapeDtypeStruct((n, d), BF16),
        compiler_params=_params("arbitrary", "arbitrary"),
        name="conformer_glu",
    )(hn, w_in, w_in, b_in, b_in)


def _conformer_out_kernel(glu_ref, halo_ref, x_ref, dww_ref, dwb_ref, lng_ref, lnb_ref, wo_ref, bo_ref, gn_ref,
                          xo_ref, hn_ref, s_ref, y_ref, *, tiles_per_seq, row_chunk):
    tm, d = glu_ref.shape
    ncol = d // LANES
    first = (pl.program_id(0) % tiles_per_seq) == 0
    halo = jnp.where(first, 0.0, halo_ref[...].astype(F32))
    cur = glu_ref[...].astype(F32)
    for c in range(ncol):
        s_ref[c, 0:CONF_HALO, :] = halo[:, c * LANES:(c + 1) * LANES]
        s_ref[c, CONF_HALO:CONF_HALO + tm, :] = cur[:, c * LANES:(c + 1) * LANES]

    base = CONF_HALO - (CONF_WIDTH - 1)

    def col_body(c, carry):
        bias = dwb_ref[c]
        for r in range(tm // row_chunk):
            acc = jnp.broadcast_to(bias, (row_chunk, LANES))
            for j in range(CONF_WIDTH):
                acc = acc + dww_ref[c, j:j + 1, :] * s_ref[c, pl.ds(r * row_chunk + base + j, row_chunk), :]
            y_ref[c, r * row_chunk:(r + 1) * row_chunk, :] = acc
        return carry

    lax.fori_loop(0, ncol, col_body, 0)

    y = jnp.concatenate([y_ref[c] for c in range(ncol)], axis=1)
    mu = jnp.mean(y, axis=-1, keepdims=True)
    yc = y - mu
    var = jnp.mean(yc * yc, axis=-1, keepdims=True)
    z = yc * lax.rsqrt(var + EPS) * lng_ref[...] + lnb_ref[...]
    a = (z * _sigmoid(z)).astype(BF16)
    out = x_ref[...] + (_dot(a, wo_ref[...]) + bo_ref[...])
    xo_ref[...] = out
    hn_ref[...] = _rms_scale(out, gn_ref[...]).astype(BF16)


def _conformer_out(glu, x, dw_w, dw_b, ln_g, ln_b, w_out, b_out, g_next, seq, tm=512, row_chunk=64):
    n, d = x.shape
    ncol = d // LANES
    halo_blocks = tm // CONF_HALO
    dww = dw_w.reshape(CONF_WIDTH, ncol, LANES).transpose(1, 0, 2)
    dwb = dw_b.reshape(ncol, 1, LANES)
    kern = functools.partial(_conformer_out_kernel, tiles_per_seq=seq // tm, row_chunk=row_chunk)
    vec = pl.BlockSpec((1, d), lambda i: (0, 0))
    return pl.pallas_call(
        kern,
        grid=(n // tm,),
        in_specs=[
            pl.BlockSpec((tm, d), lambda i: (i, 0)),
            pl.BlockSpec((CONF_HALO, d), lambda i: (jnp.maximum(i * halo_blocks - 1, 0), 0)),
            pl.BlockSpec((tm, d), lambda i: (i, 0)),
            pl.BlockSpec((ncol, CONF_WIDTH, LANES), lambda i: (0, 0, 0)),
            pl.BlockSpec((ncol, 1, LANES), lambda i: (0, 0, 0)),
            vec, vec,
            pl.BlockSpec((d, d), lambda i: (0, 0)),
            vec, vec,
        ],
        out_specs=[pl.BlockSpec((tm, d), lambda i: (i, 0)), pl.BlockSpec((tm, d), lambda i: (i, 0))],
        out_shape=[jax.ShapeDtypeStruct((n, d), F32), jax.ShapeDtypeStruct((n, d), BF16)],
        scratch_shapes=[pltpu.VMEM((ncol, tm + CONF_HALO, LANES), F32), pltpu.VMEM((ncol, tm, LANES), F32)],
        compiler_params=_params("arbitrary"),
        name="conformer_out",
    )(glu, glu, x, dww, dwb, _row(ln_g), _row(ln_b), w_out, _row(b_out), _row(g_next))


def _ffn_kernel(hn_ref, x_hbm, wg_ref, wv_ref, cwg_ref, cwv_ref, cbg_ref, cbv_ref, wd_ref, gn_ref, *rest,
                tiles_per_seq, emit_hn):
    if emit_hn:
        xo_ref, hn_out_ref, p_even_ref, p_odd_ref, carry_g_ref, carry_v_ref, x_sem = rest
    else:
        xo_ref, p_even_ref, p_odd_ref, carry_g_ref, carry_v_ref, x_sem = rest
    p_refs = (p_even_ref, p_odd_ref)
    tm = hn_ref.shape[0]
    i = pl.program_id(0)
    j = pl.program_id(1)
    nj = pl.num_programs(1) - 1
    first = (i % tiles_per_seq) == 0

    def x_copy():
        return pltpu.make_async_copy(x_hbm.at[pl.ds(pl.multiple_of(i * tm, tm), tm), :], xo_ref, x_sem)

    def shifted(u, carry, k):
        r = pltpu.roll(u, k, 0)
        row = lax.broadcasted_iota(jnp.int32, (FFN_HALO, 1), 0)
        head = jnp.where(row < k, pltpu.roll(carry, k, 0), r[0:FFN_HALO])
        return jnp.concatenate([head, r[FFN_HALO:]], axis=0)

    def causal_conv(u, carry_ref, cw_ref, cb_ref):
        carry = jnp.where(first, 0.0, carry_ref[j])
        carry_ref[j] = u[tm - FFN_HALO:, :]
        out = cb_ref[...] + cw_ref[FFN_CONV_WIDTH - 1:FFN_CONV_WIDTH, :] * u
        for k in range(1, FFN_CONV_WIDTH):
            out = out + cw_ref[FFN_CONV_WIDTH - 1 - k:FFN_CONV_WIDTH - k, :] * shifted(u, carry, k)
        return out

    def up(p_ref):
        hn = hn_ref[...]
        gate = causal_conv(_dot(hn, wg_ref[...]), carry_g_ref, cwg_ref, cbg_ref)
        val = causal_conv(_dot(hn, wv_ref[...]), carry_v_ref, cwv_ref, cbv_ref)
        p_ref[...] = (gate * _sigmoid(gate) * val).astype(BF16)

    def down(parity):
        xo_ref[...] += _dot(p_refs[parity][...], wd_ref[...])

    @pl.when(j == 0)
    def _():
        x_copy().start()
        up(p_refs[0])

    @pl.when(j == 1)
    def _():
        x_copy().wait()

    for parity in (0, 1):
        @pl.when((j > 0) & (j < nj) & (j % 2 == parity))
        def _():
            up(p_refs[parity])
            down(1 - parity)

    @pl.when(j == nj)
    def _():
        down((nj - 1) % 2)
        if emit_hn:
            hn_out_ref[...] = _rms_scale(xo_ref[...], gn_ref[...]).astype(BF16)


def _ffn(hn, x, w_up, dw_w, dw_b, w_down, g_next, seq, tm=1024, tf=512):
    n, d = x.shape
    dff = w_down.shape[0]
    nj = dff // tf
    dw_b = _row(dw_b)
    emit_hn = g_next is not None
    if not emit_hn:
        g_next = jnp.ones((d,), F32)
    kern = functools.partial(_ffn_kernel, tiles_per_seq=seq // tm, emit_hn=emit_hn)
    row_tile = pl.BlockSpec((tm, d), lambda i, j: (i, 0))
    up_blk = lambda i, j: (0, jnp.minimum(j, nj - 1))
    up_blk_val = lambda i, j: (0, jnp.minimum(j, nj - 1) + nj)
    n_out = 2 if emit_hn else 1
    res = pl.pallas_call(
        kern,
        grid=(n // tm, nj + 1),
        in_specs=[
            row_tile,
            pl.BlockSpec(memory_space=pl.ANY),
            pl.BlockSpec((d, tf), up_blk),
            pl.BlockSpec((d, tf), up_blk_val),
            pl.BlockSpec((FFN_CONV_WIDTH, tf), up_blk),
            pl.BlockSpec((FFN_CONV_WIDTH, tf), up_blk_val),
            pl.BlockSpec((1, tf), up_blk),
            pl.BlockSpec((1, tf), up_blk_val),
            pl.BlockSpec((tf, d), lambda i, j: (jnp.maximum(j - 1, 0), 0)),
            pl.BlockSpec((1, d), lambda i, j: (0, 0)),
        ],
        out_specs=[row_tile, row_tile][:n_out],
        out_shape=[jax.ShapeDtypeStruct((n, d), F32), jax.ShapeDtypeStruct((n, d), BF16)][:n_out],
        scratch_shapes=[
            pltpu.VMEM((tm, tf), BF16),
            pltpu.VMEM((tm, tf), BF16),
            pltpu.VMEM((nj, FFN_HALO, tf), F32),
            pltpu.VMEM((nj, FFN_HALO, tf), F32),
            pltpu.SemaphoreType.DMA(()),
        ],
        compiler_params=_params("arbitrary", "arbitrary"),
        name="conv_gated_mlp",
    )(hn, x, w_up, w_up, dw_w, dw_w, dw_b, dw_b, w_down, _row(g_next))
    return (res[0], res[1]) if emit_hn else (res[0], None)


def _pool_kernel(hn_ref, halo_ref, x_ref, wgrp_ref, scale_ref, gn_ref, xo_ref, hn_out_ref, s_ref, m_ref,
                 *, tiles_per_seq, row_chunk):
    tm, d = hn_ref.shape
    gdim = d // len(POOL_WINDOWS)
    tile = pl.program_id(0) % tiles_per_seq
    first = tile == 0
    s_ref[0:POOL_HALO, :] = jnp.where(first, 0.0, halo_ref[...].astype(F32))
    s_ref[POOL_HALO:POOL_HALO + tm, :] = hn_ref[...].astype(F32)

    t = tile * tm + lax.broadcasted_iota(jnp.int32, (tm, 1), 0)
    for g, w in enumerate(POOL_WINDOWS):
        cols = slice(g * gdim, (g + 1) * gdim)
        inv_count = 1.0 / jnp.minimum(t + 1, w).astype(F32)
        for r in range(tm // row_chunk):
            rows = r * row_chunk
            cur = s_ref[POOL_HALO + rows:POOL_HALO + rows + row_chunk, cols]
            acc = cur
            for k in range(1, w):
                acc = acc + s_ref[pl.ds(POOL_HALO + rows - k, row_chunk), cols]
            mixed = acc * inv_count[rows:rows + row_chunk] - cur
            m_ref[rows:rows + row_chunk, cols] = mixed.astype(BF16)

    y = jnp.concatenate(
        [_dot(m_ref[:, g * gdim:(g + 1) * gdim], wgrp_ref[g]) for g in range(len(POOL_WINDOWS))], axis=1)
    out = x_ref[...] + y * scale_ref[...]
    xo_ref[...] = out
    hn_out_ref[...] = _rms_scale(out, gn_ref[...]).astype(BF16)


def _pool_mixer(hn, x, w_group, scale, g_next, seq, tm=512, row_chunk=64):
    n, d = x.shape
    ng, gdim, _ = w_group.shape
    halo_blocks = tm // POOL_HALO
    kern = functools.partial(_pool_kernel, tiles_per_seq=seq // tm, row_chunk=row_chunk)
    vec = pl.BlockSpec((1, d), lambda i: (0, 0))
    return pl.pallas_call(
        kern,
        grid=(n // tm,),
        in_specs=[
            pl.BlockSpec((tm, d), lambda i: (i, 0)),
            pl.BlockSpec((POOL_HALO, d), lambda i: (jnp.maximum(i * halo_blocks - 1, 0), 0)),
            pl.BlockSpec((tm, d), lambda i: (i, 0)),
            pl.BlockSpec((ng, gdim, gdim), lambda i: (0, 0, 0)),
            vec, vec,
        ],
        out_specs=[pl.BlockSpec((tm, d), lambda i: (i, 0)), pl.BlockSpec((tm, d), lambda i: (i, 0))],
        out_shape=[jax.ShapeDtypeStruct((n, d), F32), jax.ShapeDtypeStruct((n, d), BF16)],
        scratch_shapes=[pltpu.VMEM((tm + POOL_HALO, d), F32), pltpu.VMEM((tm, d), BF16)],
        compiler_params=_params("arbitrary"),
        name="pool_mixer",
    )(hn, hn, x, w_group, _row(scale), _row(g_next))


def _qkv_kernel(hn_ref, pos_ref, w_ref, e_ref, et_ref, gain_ref, invf_ref, q_ref, k_ref, v_ref):
    d_q = q_ref.shape[1]
    d_kv = N_KV_HEADS * HEAD_DIM
    d_qk = d_q + d_kv
    acc = _dot(hn_ref[...], w_ref[...])
    qk = acc[:, :d_qk]

    ms = _dot((qk * qk).astype(BF16), e_ref[...]) * (1.0 / HEAD_DIM)
    scale = lax.rsqrt(ms + EPS)
    scale_hi = scale.astype(BF16)
    scale_lo = (scale - scale_hi.astype(F32)).astype(BF16)
    scale_b = _dot(scale_hi, et_ref[...]) + _dot(scale_lo, et_ref[...])
    qkn = qk * scale_b * gain_ref[...]

    half = ROT_DIM // 2
    lane = lax.broadcasted_iota(jnp.int32, (1, LANES), 1) % HEAD_DIM
    ang = pos_ref[...].astype(F32) * invf_ref[...]
    cos = jnp.cos(ang)
    sin = jnp.sin(ang)
    c_self = jnp.where(lane < ROT_DIM, cos, 1.0)
    c_up = jnp.where(lane < half, -sin, 0.0)
    c_dn = jnp.where((lane >= half) & (lane < ROT_DIM), sin, 0.0)
    cols = []
    for c in range(d_qk // LANES):
        xc = qkn[:, c * LANES:(c + 1) * LANES]
        cols.append(xc * c_self + pltpu.roll(xc, LANES - half, 1) * c_up + pltpu.roll(xc, half, 1) * c_dn)

    nq = d_q // LANES
    sm_scale = 1.0 / (HEAD_DIM ** 0.5)
    q_ref[...] = (jnp.concatenate(cols[:nq], axis=1) * sm_scale).astype(BF16)

    lo = lax.broadcasted_iota(jnp.int32, (1, LANES), 1) < HEAD_DIM

    def dup(xc):
        r = pltpu.roll(xc, HEAD_DIM, 1)
        return [jnp.where(lo, xc, r), jnp.where(lo, r, xc)]

    kd, vd = [], []
    for c in range(d_kv // LANES):
        kd += dup(cols[nq + c])
        vd += dup(acc[:, d_qk + c * LANES:d_qk + (c + 1) * LANES])
    k_ref[...] = jnp.concatenate(kd, axis=1).astype(BF16)
    v_ref[...] = jnp.concatenate(vd, axis=1).astype(BF16)


def _qkv(hn, positions, w_qkv, q_gain, k_gain, tm=512):
    n, d = hn.shape
    d_q = N_HEADS * HEAD_DIM
    d_kv = N_KV_HEADS * HEAD_DIM
    d_qk = d_q + d_kv
    d_all = d_q + 2 * d_kv
    head_of_col = jnp.arange(d_qk) // HEAD_DIM
    e = (head_of_col[:, None] == jnp.arange(LANES)[None, :]).astype(BF16)
    et = e.T
    gain = jnp.concatenate([jnp.tile(q_gain, N_HEADS), jnp.tile(k_gain, N_KV_HEADS)]).astype(F32)
    inv_freq = ROPE_THETA ** (-jnp.arange(0, ROT_DIM, 2, dtype=F32) / ROT_DIM)
    lane = jnp.arange(LANES) % HEAD_DIM
    invf = jnp.where(lane < ROT_DIM, inv_freq[lane % (ROT_DIM // 2)], 0.0).astype(F32)
    const = lambda shape: pl.BlockSpec(shape, lambda i: (0, 0))
    return pl.pallas_call(
        _qkv_kernel,
        grid=(n // tm,),
        in_specs=[
            pl.BlockSpec((tm, d), lambda i: (i, 0)),
            pl.BlockSpec((tm, 1), lambda i: (i, 0)),
            const((d, d_all)), const((d_qk, LANES)), const((LANES, d_qk)), const((1, d_qk)), const((1, LANES)),
        ],
        out_specs=[
            pl.BlockSpec((tm, d_q), lambda i: (i, 0)),
            pl.BlockSpec((tm, 2 * d_kv), lambda i: (i, 0)),
            pl.BlockSpec((tm, 2 * d_kv), lambda i: (i, 0)),
        ],
        out_shape=[
            jax.ShapeDtypeStruct((n, d_q), BF16),
            jax.ShapeDtypeStruct((n, 2 * d_kv), BF16),
            jax.ShapeDtypeStruct((n, 2 * d_kv), BF16),
        ],
        compiler_params=_params("arbitrary"),
        name="qkv_norm_rope",
    )(hn, positions.reshape(n, 1), w_qkv, e, et, _row(gain), _row(invf))


def _attn_kernel(sinks_ref, q_ref, kc_ref, kp_ref, vc_ref, vp_ref, o_ref):
    blk = q_ref.shape[0]
    n = pl.program_id(1)
    lo = lax.broadcasted_iota(jnp.int32, (1, LANES), 1) < HEAD_DIM
    rows = GROUP_SIZE * blk
    qi = lax.broadcasted_iota(jnp.int32, (rows, 2 * blk), 0) % blk
    kj = lax.broadcasted_iota(jnp.int32, (rows, 2 * blk), 1)
    valid = (kj > qi) & (kj <= qi + blk) & ((kj >= blk) | (n > 0))
    bias = jnp.where(valid, 0.0, NEG_BIG)
    contract_last = (((1,), (1,)), ((), ()))
    for g in range(N_KV_HEADS):
        gcols = slice(g * LANES, (g + 1) * LANES)
        kband = jnp.concatenate([kp_ref[:, gcols], kc_ref[:, gcols]], axis=0)
        vband = jnp.concatenate([vp_ref[:, gcols], vc_ref[:, gcols]], axis=0)
        parts, sink_parts = [], []
        for h in range(GROUP_SIZE):
            pair = g * (GROUP_SIZE // 2) + h // 2
            qp = q_ref[:, pair * LANES:(pair + 1) * LANES]
            keep = lo if h % 2 == 0 else jnp.logical_not(lo)
            parts.append(jnp.where(keep, qp, jnp.zeros_like(qp)))
            sink_parts.append(jnp.full((blk, 1), sinks_ref[g * GROUP_SIZE + h], F32))
        qs = jnp.concatenate(parts, axis=0)
        sink = jnp.concatenate(sink_parts, axis=0)
        s = lax.dot_general(qs, kband, contract_last, preferred_element_type=F32) + bias
        m = jnp.maximum(jnp.max(s, axis=-1, keepdims=True), sink)
        p = jnp.exp(s - m)
        denom = jnp.sum(p, axis=-1, keepdims=True) + jnp.exp(sink - m)
        o = _dot(p.astype(BF16), vband) * (1.0 / denom)
        for h2 in range(GROUP_SIZE // 2):
            even = o[(2 * h2) * blk:(2 * h2 + 1) * blk]
            odd = o[(2 * h2 + 1) * blk:(2 * h2 + 2) * blk]
            pair = g * (GROUP_SIZE // 2) + h2
            o_ref[:, pair * LANES:(pair + 1) * LANES] = jnp.where(lo, even, odd).astype(BF16)


def _attention(q, kd, vd, sinks, batch, seq):
    n, d_q = q.shape
    blk = ATT_BLOCK
    nb = seq // blk
    w_kv = kd.shape[1]
    cur = lambda b, i: (b * nb + i, 0)
    prev = lambda b, i: (b * nb + jnp.maximum(i - 1, 0), 0)
    return pl.pallas_call(
        _attn_kernel,
        grid=(batch, nb),
        in_specs=[
            pl.BlockSpec(memory_space=pltpu.SMEM),
            pl.BlockSpec((blk, d_q), cur),
            pl.BlockSpec((blk, w_kv), cur),
            pl.BlockSpec((blk, w_kv), prev),
            pl.BlockSpec((blk, w_kv), cur),
            pl.BlockSpec((blk, w_kv), prev),
        ],
        out_specs=pl.BlockSpec((blk, d_q), cur),
        out_shape=jax.ShapeDtypeStruct((n, d_q), BF16),
        compiler_params=_params("arbitrary", "arbitrary"),
        name="banded_sink_attention",
    )(sinks.astype(F32), q, kd, kd, vd, vd)


def _oproj_kernel(a_ref, x_ref, w_ref, gn_ref, xo_ref, hn_ref):
    out = x_ref[...] + _dot(a_ref[...], w_ref[...])
    xo_ref[...] = out
    hn_ref[...] = _rms_scale(out, gn_ref[...]).astype(BF16)


def _oproj(a, x, w, g_next, tm=512):
    n, d = x.shape
    k = a.shape[1]
    return pl.pallas_call(
        _oproj_kernel,
        grid=(n // tm,),
        in_specs=[
            pl.BlockSpec((tm, k), lambda i: (i, 0)),
            pl.BlockSpec((tm, d), lambda i: (i, 0)),
            pl.BlockSpec((k, d), lambda i: (0, 0)),
            pl.BlockSpec((1, d), lambda i: (0, 0)),
        ],
        out_specs=[pl.BlockSpec((tm, d), lambda i: (i, 0)), pl.BlockSpec((tm, d), lambda i: (i, 0))],
        out_shape=[jax.ShapeDtypeStruct((n, d), F32), jax.ShapeDtypeStruct((n, d), BF16)],
        compiler_params=_params("arbitrary"),
        name="attn_out_proj",
    )(a, x, w, _row(g_next))


def kernel(x, positions, l0_norm_g, l0_a_w_in, l0_a_b_in, l0_a_dw_w, l0_a_dw_b, l0_a_ln_g, l0_a_ln_b, l0_a_w_out, l0_a_b_out, l0_ffn_norm_g, l0_ffn_w_up, l0_ffn_dw_w, l0_ffn_dw_b, l0_ffn_w_down, l1_norm_g, l1_b_w_group, l1_b_scale, l1_ffn_norm_g, l1_ffn_w_up, l1_ffn_dw_w, l1_ffn_dw_b, l1_ffn_w_down, l2_norm_g, l2_c_w_qkv, l2_c_q_norm_g, l2_c_k_norm_g, l2_c_sinks, l2_c_w_o, l2_ffn_norm_g, l2_ffn_w_up, l2_ffn_dw_w, l2_ffn_dw_b, l2_ffn_w_down, l3_norm_g, l3_a_w_in, l3_a_b_in, l3_a_dw_w, l3_a_dw_b, l3_a_ln_g, l3_a_ln_b, l3_a_w_out, l3_a_b_out, l3_ffn_norm_g, l3_ffn_w_up, l3_ffn_dw_w, l3_ffn_dw_b, l3_ffn_w_down):
    batch, seq, d = x.shape
    n = batch * seq
    xs = x.reshape(n, d)
    bf = lambda w: w.astype(BF16)

    def conformer(xs, hn, w_in, b_in, dw_w, dw_b, ln_g, ln_b, w_out, b_out, g_next):
        glu = _conformer_glu(hn, bf(w_in), b_in)
        return _conformer_out(glu, xs, dw_w, dw_b, ln_g, ln_b, bf(w_out), b_out, g_next, seq)

    def ffn(xs, hn, w_up, dw_w, dw_b, w_down, g_next):
        return _ffn(hn, xs, bf(w_up), dw_w, dw_b, bf(w_down), g_next, seq)

    hn = _prenorm(xs, l0_norm_g)
    xs, hn = conformer(xs, hn, l0_a_w_in, l0_a_b_in, l0_a_dw_w, l0_a_dw_b, l0_a_ln_g, l0_a_ln_b, l0_a_w_out,
                       l0_a_b_out, l0_ffn_norm_g)
    xs, hn = ffn(xs, hn, l0_ffn_w_up, l0_ffn_dw_w, l0_ffn_dw_b, l0_ffn_w_down, l1_norm_g)
    xs, hn = _pool_mixer(hn, xs, bf(l1_b_w_group), l1_b_scale, l1_ffn_norm_g, seq)
    xs, hn = ffn(xs, hn, l1_ffn_w_up, l1_ffn_dw_w, l1_ffn_dw_b, l1_ffn_w_down, l2_norm_g)
    q, kd, vd = _qkv(hn, positions, bf(l2_c_w_qkv), l2_c_q_norm_g, l2_c_k_norm_g)
    att = _attention(q, kd, vd, l2_c_sinks, batch, seq)
    xs, hn = _oproj(att, xs, bf(l2_c_w_o), l2_ffn_norm_g)
    xs, hn = ffn(xs, hn, l2_ffn_w_up, l2_ffn_dw_w, l2_ffn_dw_b, l2_ffn_w_down, l3_norm_g)
    xs, hn = conformer(xs, hn, l3_a_w_in, l3_a_b_in, l3_a_dw_w, l3_a_dw_b, l3_a_ln_g, l3_a_ln_b, l3_a_w_out,
                       l3_a_b_out, l3_ffn_norm_g)
    xs, _ = ffn(xs, hn, l3_ffn_w_up, l3_ffn_dw_w, l3_ffn_dw_b, l3_ffn_w_down, None)
    return xs.reshape(batch, seq, d)
```

```python
import functools

import jax
import jax.numpy as jnp
from jax import lax
from jax.experimental import pallas as pl
from jax.experimental.pallas import tpu as pltpu

F32 = jnp.float32
BF16 = jnp.bfloat16

EPS = 1e-6
LANES = 128
HEAD_DIM = 64
N_HEADS = 32
N_KV_HEADS = 4
GROUP_SIZE = N_HEADS // N_KV_HEADS
ATT_BLOCK = 128
ROT_DIM = 16
ROPE_THETA = 500000.0
POOL_WINDOWS = (2, 4, 8, 16)
CONF_WIDTH = 31
FFN_CONV_WIDTH = 3
CONF_HALO = 32
POOL_HALO = 16
FFN_HALO = 8
NEG_BIG = -1e30

VMEM_LIMIT_BYTES = 56 * 1024 * 1024


def _params(*semantics):
    return pltpu.CompilerParams(dimension_semantics=semantics, vmem_limit_bytes=VMEM_LIMIT_BYTES)


def _sigmoid(x):
    return 1.0 / (1.0 + jnp.exp(-x))


def _rms_scale(x, g):
    ms = jnp.mean(x * x, axis=-1, keepdims=True)
    return x * lax.rsqrt(ms + EPS) * g


def _dot(a, b):
    return jnp.dot(a, b, preferred_element_type=F32)


def _row(v):
    return v.reshape(1, -1)


def _prenorm_kernel(x_ref, g_ref, hn_ref):
    hn_ref[...] = _rms_scale(x_ref[...], g_ref[...]).astype(BF16)


def _prenorm(x, g, tm=1024):
    n, d = x.shape
    return pl.pallas_call(
        _prenorm_kernel,
        grid=(n // tm,),
        in_specs=[pl.BlockSpec((tm, d), lambda i: (i, 0)), pl.BlockSpec((1, d), lambda i: (0, 0))],
        out_specs=pl.BlockSpec((tm, d), lambda i: (i, 0)),
        out_shape=jax.ShapeDtypeStruct((n, d), BF16),
        compiler_params=_params("arbitrary"),
        name="prenorm",
    )(x, _row(g))


def _glu_kernel(hn_ref, wa_ref, wg_ref, ba_ref, bg_ref, o_ref):
    hn = hn_ref[...]
    a = _dot(hn, wa_ref[...]) + ba_ref[...]
    g = _dot(hn, wg_ref[...]) + bg_ref[...]
    o_ref[...] = (a * _sigmoid(g)).astype(BF16)


def _conformer_glu(hn, w_in, b_in, tm=1024, tn=1024):
    n, d = hn.shape
    nb = d // tn
    b_in = _row(b_in)
    return pl.pallas_call(
        _glu_kernel,
        grid=(n // tm, nb),
        in_specs=[
            pl.BlockSpec((tm, d), lambda i, j: (i, 0)),
            pl.BlockSpec((d, tn), lambda i, j: (0, j)),
            pl.BlockSpec((d, tn), lambda i, j: (0, j + nb)),
            pl.BlockSpec((1, tn), lambda i, j: (0, j)),
            pl.BlockSpec((1, tn), lambda i, j: (0, j + nb)),
        ],
        out_specs=pl.BlockSpec((tm, tn), lambda i, j: (i, j)),
        out_shape=jax.ShapeDtypeStruct((n, d), BF16),
        compiler_params=_params("arbitrary", "arbitrary"),
        name="conformer_glu",
    )(hn, w_in, w_in, b_in, b_in)


def _conformer_out_kernel(glu_ref, halo_ref, x_ref, dww_ref, dwb_ref, lng_ref, lnb_ref, wo_ref, bo_ref, gn_ref,
                          xo_ref, hn_ref, s_ref, y_ref, *, tiles_per_seq, row_chunk):
    tm, d = glu_ref.shape
    ncol = d // LANES
    first = (pl.program_id(0) % tiles_per_seq) == 0
    halo = jnp.where(first, 0.0, halo_ref[...].astype(F32))
    cur = glu_ref[...].astype(F32)
    for c in range(ncol):
        s_ref[c, 0:CONF_HALO, :] = halo[:, c * LANES:(c + 1) * LANES]
        s_ref[c, CONF_HALO:CONF_HALO + tm, :] = cur[:, c * LANES:(c + 1) * LANES]

    base = CONF_HALO - (CONF_WIDTH - 1)

    def col_body(c, carry):
        bias = dwb_ref[c]
        for r in range(tm // row_chunk):
            acc = jnp.broadcast_to(bias, (row_chunk, LANES))
            for j in range(CONF_WIDTH):
                acc = acc + dww_ref[c, j:j + 1, :] * s_ref[c, pl.ds(r * row_chunk + base + j, row_chunk), :]
            y_ref[c, r * row_chunk:(r + 1) * row_chunk, :] = acc
        return carry

    lax.fori_loop(0, ncol, col_body, 0)

    y = jnp.concatenate([y_ref[c] for c in range(ncol)], axis=1)
    mu = jnp.mean(y, axis=-1, keepdims=True)
    yc = y - mu
    var = jnp.mean(yc * yc, axis=-1, keepdims=True)
    z = yc * lax.rsqrt(var + EPS) * lng_ref[...] + lnb_ref[...]
    a = (z * _sigmoid(z)).astype(BF16)
    out = x_ref[...] + (_dot(a, wo_ref[...]) + bo_ref[...])
    xo_ref[...] = out
    hn_ref[...] = _rms_scale(out, gn_ref[...]).astype(BF16)


def _conformer_out(glu, x, dw_w, dw_b, ln_g, ln_b, w_out, b_out, g_next, seq, tm=512, row_chunk=64):
    n, d = x.shape
    ncol = d // LANES
    halo_blocks = tm // CONF_HALO
    dww = dw_w.reshape(CONF_WIDTH, ncol, LANES).transpose(1, 0, 2)
    dwb = dw_b.reshape(ncol, 1, LANES)
    kern = functools.partial(_conformer_out_kernel, tiles_per_seq=seq // tm, row_chunk=row_chunk)
    vec = pl.BlockSpec((1, d), lambda i: (0, 0))
    return pl.pallas_call(
        kern,
        grid=(n // tm,),
        in_specs=[
            pl.BlockSpec((tm, d), lambda i: (i, 0)),
            pl.BlockSpec((CONF_HALO, d), lambda i: (jnp.maximum(i * halo_blocks - 1, 0), 0)),
            pl.BlockSpec((tm, d), lambda i: (i, 0)),
            pl.BlockSpec((ncol, CONF_WIDTH, LANES), lambda i: (0, 0, 0)),
            pl.BlockSpec((ncol, 1, LANES), lambda i: (0, 0, 0)),
            vec, vec,
            pl.BlockSpec((d, d), lambda i: (0, 0)),
            vec, vec,
        ],
        out_specs=[pl.BlockSpec((tm, d), lambda i: (i, 0)), pl.BlockSpec((tm, d), lambda i: (i, 0))],
        out_shape=[jax.ShapeDtypeStruct((n, d), F32), jax.ShapeDtypeStruct((n, d), BF16)],
        scratch_shapes=[pltpu.VMEM((ncol, tm + CONF_HALO, LANES), F32), pltpu.VMEM((ncol, tm, LANES), F32)],
        compiler_params=_params("arbitrary"),
        name="conformer_out",
    )(glu, glu, x, dww, dwb, _row(ln_g), _row(ln_b), w_out, _row(b_out), _row(g_next))


def _ffn_kernel(hn_ref, x_hbm, wg_ref, wv_ref, cwg_ref, cwv_ref, cbg_ref, cbv_ref, wd_ref, gn_ref, *rest,
                tiles_per_seq, emit_hn):
    if emit_hn:
        xo_ref, hn_out_ref, p_even_ref, p_odd_ref, carry_g_ref, carry_v_ref, x_sem = rest
    else:
        xo_ref, p_even_ref, p_odd_ref, carry_g_ref, carry_v_ref, x_sem = rest
    p_refs = (p_even_ref, p_odd_ref)
    tm = hn_ref.shape[0]
    i = pl.program_id(0)
    j = pl.program_id(1)
    nj = pl.num_programs(1) - 1
    first = (i % tiles_per_seq) == 0

    def x_copy():
        return pltpu.make_async_copy(x_hbm.at[pl.ds(pl.multiple_of(i * tm, tm), tm), :], xo_ref, x_sem)

    def shifted(u, carry, k):
        r = pltpu.roll(u, k, 0)
        row = lax.broadcasted_iota(jnp.int32, (FFN_HALO, 1), 0)
        head = jnp.where(row < k, pltpu.roll(carry, k, 0), r[0:FFN_HALO])
        return jnp.concatenate([head, r[FFN_HALO:]], axis=0)

    def causal_conv(u, carry_ref, cw_ref, cb_ref):
        carry = jnp.where(first, 0.0, carry_ref[j])
        carry_ref[j] = u[tm - FFN_HALO:, :]
        out = cb_ref[...] + cw_ref[FFN_CONV_WIDTH - 1:FFN_CONV_WIDTH, :] * u
        for k in range(1, FFN_CONV_WIDTH):
            out = out + cw_ref[FFN_CONV_WIDTH - 1 - k:FFN_CONV_WIDTH - k, :] * shifted(u, carry, k)
        return out

    def up(p_ref):
        hn = hn_ref[...]
        gate = causal_conv(_dot(hn, wg_ref[...]), carry_g_ref, cwg_ref, cbg_ref)
        val = causal_conv(_dot(hn, wv_ref[...]), carry_v_ref, cwv_ref, cbv_ref)
        p_ref[...] = (gate * _sigmoid(gate) * val).astype(BF16)

    def down(parity):
        xo_ref[...] += _dot(p_refs[parity][...], wd_ref[...])

    @pl.when(j == 0)
    def _():
        x_copy().start()
        up(p_refs[0])

    @pl.when(j == 1)
    def _():
        x_copy().wait()

    for parity in (0, 1):
        @pl.when((j > 0) & (j < nj) & (j % 2 == parity))
        def _():
            up(p_refs[parity])
            down(1 - parity)

    @pl.when(j == nj)
    def _():
        down((nj - 1) % 2)
        if emit_hn:
            hn_out_ref[...] = _rms_scale(xo_ref[...], gn_ref[...]).astype(BF16)


def _ffn(hn, x, w_up, dw_w, dw_b, w_down, g_next, seq, tm=1024, tf=512):
    n, d = x.shape
    dff = w_down.shape[0]
    nj = dff // tf
    dw_b = _row(dw_b)
    emit_hn = g_next is not None
    if not emit_hn:
        g_next = jnp.ones((d,), F32)
    kern = functools.partial(_ffn_kernel, tiles_per_seq=seq // tm, emit_hn=emit_hn)
    row_tile = pl.BlockSpec((tm, d), lambda i, j: (i, 0))
    up_blk = lambda i, j: (0, jnp.minimum(j, nj - 1))
    up_blk_val = lambda i, j: (0, jnp.minimum(j, nj - 1) + nj)
    n_out = 2 if emit_hn else 1
    res = pl.pallas_call(
        kern,
        grid=(n // tm, nj + 1),
        in_specs=[
            row_tile,
            pl.BlockSpec(memory_space=pl.ANY),
            pl.BlockSpec((d, tf), up_blk),
            pl.BlockSpec((d, tf), up_blk_val),
            pl.BlockSpec((FFN_CONV_WIDTH, tf), up_blk),
            pl.BlockSpec((FFN_CONV_WIDTH, tf), up_blk_val),
            pl.BlockSpec((1, tf), up_blk),
            pl.BlockSpec((1, tf), up_blk_val),
            pl.BlockSpec((tf, d), lambda i, j: (jnp.maximum(j - 1, 0), 0)),
            pl.BlockSpec((1, d), lambda i, j: (0, 0)),
        ],
        out_specs=[row_tile, row_tile][:n_out],
        out_shape=[jax.ShapeDtypeStruct((n, d), F32), jax.ShapeDtypeStruct((n, d), BF16)][:n_out],
        scratch_shapes=[
            pltpu.VMEM((tm, tf), BF16),
            pltpu.VMEM((tm, tf), BF16),
            pltpu.VMEM((nj, FFN_HALO, tf), F32),
            pltpu.VMEM((nj, FFN_HALO, tf), F32),
            pltpu.SemaphoreType.DMA(()),
        ],
        compiler_params=_params("arbitrary", "arbitrary"),
        name="conv_gated_mlp",
    )(hn, x, w_up, w_up, dw_w, dw_w, dw_b, dw_b, w_down, _row(g_next))
    return (res[0], res[1]) if emit_hn else (res[0], None)


def _pool_kernel(hn_ref, halo_ref, x_ref, wgrp_ref, scale_ref, gn_ref, xo_ref, hn_out_ref, s_ref, m_ref,
                 *, tiles_per_seq, row_chunk):
    tm, d = hn_ref.shape
    gdim = d // len(POOL_WINDOWS)
    tile = pl.program_id(0) % tiles_per_seq
    first = tile == 0
    halo = jnp.where(first, 0.0, halo_ref[...].astype(F32))
    cur_tile = hn_ref[...].astype(F32)
    for c in range(d // LANES):
        s_ref[c, 0:POOL_HALO, :] = halo[:, c * LANES:(c + 1) * LANES]
        s_ref[c, POOL_HALO:POOL_HALO + tm, :] = cur_tile[:, c * LANES:(c + 1) * LANES]

    t = tile * tm + lax.broadcasted_iota(jnp.int32, (tm, 1), 0)
    for g, w in enumerate(POOL_WINDOWS):
        inv_count = 1.0 / jnp.minimum(t + 1, w).astype(F32)
        for c in range(g * gdim // LANES, (g + 1) * gdim // LANES):
            cols = slice(c * LANES, (c + 1) * LANES)
            for r in range(tm // row_chunk):
                rows = r * row_chunk
                cur = s_ref[c, POOL_HALO + rows:POOL_HALO + rows + row_chunk, :]
                acc = cur
                for k in range(1, w):
                    acc = acc + s_ref[c, pl.ds(POOL_HALO + rows - k, row_chunk), :]
                mixed = acc * inv_count[rows:rows + row_chunk] - cur
                m_ref[rows:rows + row_chunk, cols] = mixed.astype(BF16)

    y = jnp.concatenate(
        [_dot(m_ref[:, g * gdim:(g + 1) * gdim], wgrp_ref[g]) for g in range(len(POOL_WINDOWS))], axis=1)
    out = x_ref[...] + y * scale_ref[...]
    xo_ref[...] = out
    hn_out_ref[...] = _rms_scale(out, gn_ref[...]).astype(BF16)


def _pool_mixer(hn, x, w_group, scale, g_next, seq, tm=512, row_chunk=64):
    n, d = x.shape
    ng, gdim, _ = w_group.shape
    halo_blocks = tm // POOL_HALO
    kern = functools.partial(_pool_kernel, tiles_per_seq=seq // tm, row_chunk=row_chunk)
    vec = pl.BlockSpec((1, d), lambda i: (0, 0))
    return pl.pallas_call(
        kern,
        grid=(n // tm,),
        in_specs=[
            pl.BlockSpec((tm, d), lambda i: (i, 0)),
            pl.BlockSpec((POOL_HALO, d), lambda i: (jnp.maximum(i * halo_blocks - 1, 0), 0)),
            pl.BlockSpec((tm, d), lambda i: (i, 0)),
            pl.BlockSpec((ng, gdim, gdim), lambda i: (0, 0, 0)),
            vec, vec,
        ],
        out_specs=[pl.BlockSpec((tm, d), lambda i: (i, 0)), pl.BlockSpec((tm, d), lambda i: (i, 0))],
        out_shape=[jax.ShapeDtypeStruct((n, d), F32), jax.ShapeDtypeStruct((n, d), BF16)],
        scratch_shapes=[pltpu.VMEM((d // LANES, tm + POOL_HALO, LANES), F32), pltpu.VMEM((tm, d), BF16)],
        compiler_params=_params("arbitrary"),
        name="pool_mixer",
    )(hn, hn, x, w_group, _row(scale), _row(g_next))


def _qkv_kernel(hn_ref, pos_ref, w_ref, e_ref, et_ref, gain_ref, invf_ref, q_ref, k_ref, v_ref, *, sub_tiles):
    tm, d_q = q_ref.shape
    d_kv = N_KV_HEADS * HEAD_DIM
    d_qk = d_q + d_kv
    half = ROT_DIM // 2
    lane = lax.broadcasted_iota(jnp.int32, (1, LANES), 1) % HEAD_DIM
    lo = lax.broadcasted_iota(jnp.int32, (1, LANES), 1) < HEAD_DIM
    nq = d_q // LANES
    sm_scale = 1.0 / (HEAD_DIM ** 0.5)

    def dup(xc):
        r = pltpu.roll(xc, HEAD_DIM, 1)
        return [jnp.where(lo, xc, r), jnp.where(lo, r, xc)]

    sub = tm // sub_tiles
    for m in range(sub_tiles):
        rows = slice(m * sub, (m + 1) * sub)
        acc = _dot(hn_ref[rows, :], w_ref[...])
        qk = acc[:, :d_qk]

        ms = _dot((qk * qk).astype(BF16), e_ref[...]) * (1.0 / HEAD_DIM)
        scale = lax.rsqrt(ms + EPS)
        scale_hi = scale.astype(BF16)
        scale_lo = (scale - scale_hi.astype(F32)).astype(BF16)
        scale_b = _dot(scale_hi, et_ref[...]) + _dot(scale_lo, et_ref[...])
        qkn = qk * scale_b * gain_ref[...]

        ang = pos_ref[rows, :].astype(F32) * invf_ref[...]
        cos = jnp.cos(ang)
        sin = jnp.sin(ang)
        c_self = jnp.where(lane < ROT_DIM, cos, 1.0)
        c_up = jnp.where(lane < half, -sin, 0.0)
        c_dn = jnp.where((lane >= half) & (lane < ROT_DIM), sin, 0.0)
        cols = []
        for c in range(d_qk // LANES):
            xc = qkn[:, c * LANES:(c + 1) * LANES]
            cols.append(xc * c_self + pltpu.roll(xc, LANES - half, 1) * c_up + pltpu.roll(xc, half, 1) * c_dn)

        q_ref[rows, :] = (jnp.concatenate(cols[:nq], axis=1) * sm_scale).astype(BF16)

        kd, vd = [], []
        for c in range(d_kv // LANES):
            kd += dup(cols[nq + c])
            vd += dup(acc[:, d_qk + c * LANES:d_qk + (c + 1) * LANES])
        k_ref[rows, :] = jnp.concatenate(kd, axis=1).astype(BF16)
        v_ref[rows, :] = jnp.concatenate(vd, axis=1).astype(BF16)


def _qkv(hn, positions, w_qkv, q_gain, k_gain, tm=1024, sub_tiles=4):
    n, d = hn.shape
    d_q = N_HEADS * HEAD_DIM
    d_kv = N_KV_HEADS * HEAD_DIM
    d_qk = d_q + d_kv
    d_all = d_q + 2 * d_kv
    head_of_col = jnp.arange(d_qk) // HEAD_DIM
    e = (head_of_col[:, None] == jnp.arange(LANES)[None, :]).astype(BF16)
    et = e.T
    gain = jnp.concatenate([jnp.tile(q_gain, N_HEADS), jnp.tile(k_gain, N_KV_HEADS)]).astype(F32)
    inv_freq = ROPE_THETA ** (-jnp.arange(0, ROT_DIM, 2, dtype=F32) / ROT_DIM)
    lane = jnp.arange(LANES) % HEAD_DIM
    invf = jnp.where(lane < ROT_DIM, inv_freq[lane % (ROT_DIM // 2)], 0.0).astype(F32)
    const = lambda shape: pl.BlockSpec(shape, lambda i: (0, 0), pipeline_mode=pl.Buffered(1))
    return pl.pallas_call(
        functools.partial(_qkv_kernel, sub_tiles=sub_tiles),
        grid=(n // tm,),
        in_specs=[
            pl.BlockSpec((tm, d), lambda i: (i, 0)),
            pl.BlockSpec((tm, 1), lambda i: (i, 0)),
            const((d, d_all)), const((d_qk, LANES)), const((LANES, d_qk)), const((1, d_qk)), const((1, LANES)),
        ],
        out_specs=[
            pl.BlockSpec((tm, d_q), lambda i: (i, 0)),
            pl.BlockSpec((tm, 2 * d_kv), lambda i: (i, 0)),
            pl.BlockSpec((tm, 2 * d_kv), lambda i: (i, 0)),
        ],
        out_shape=[
            jax.ShapeDtypeStruct((n, d_q), BF16),
            jax.ShapeDtypeStruct((n, 2 * d_kv), BF16),
            jax.ShapeDtypeStruct((n, 2 * d_kv), BF16),
        ],
        compiler_params=_params("arbitrary"),
        name="qkv_norm_rope",
    )(hn, positions.reshape(n, 1), w_qkv, e, et, _row(gain), _row(invf))


def _attn_kernel(sinks_ref, q_ref, kc_ref, kp_ref, vc_ref, vp_ref, o_ref):
    blk = q_ref.shape[0]
    n = pl.program_id(1)
    lo = lax.broadcasted_iota(jnp.int32, (1, LANES), 1) < HEAD_DIM
    rows = GROUP_SIZE * blk
    qi = lax.broadcasted_iota(jnp.int32, (rows, 2 * blk), 0) % blk
    kj = lax.broadcasted_iota(jnp.int32, (rows, 2 * blk), 1)
    valid = (kj > qi) & (kj <= qi + blk) & ((kj >= blk) | (n > 0))
    bias = jnp.where(valid, 0.0, NEG_BIG)
    contract_last = (((1,), (1,)), ((), ()))
    for g in range(N_KV_HEADS):
        gcols = slice(g * LANES, (g + 1) * LANES)
        kband = jnp.concatenate([kp_ref[:, gcols], kc_ref[:, gcols]], axis=0)
        vband = jnp.concatenate([vp_ref[:, gcols], vc_ref[:, gcols]], axis=0)
        parts, sink_parts = [], []
        for h in range(GROUP_SIZE):
            pair = g * (GROUP_SIZE // 2) + h // 2
            qp = q_ref[:, pair * LANES:(pair + 1) * LANES]
            keep = lo if h % 2 == 0 else jnp.logical_not(lo)
            parts.append(jnp.where(keep, qp, jnp.zeros_like(qp)))
            sink_parts.append(jnp.full((blk, 1), sinks_ref[g * GROUP_SIZE + h], F32))
        qs = jnp.concatenate(parts, axis=0)
        sink = jnp.concatenate(sink_parts, axis=0)
        s = lax.dot_general(qs, kband, contract_last, preferred_element_type=F32) + bias
        m = jnp.maximum(jnp.max(s, axis=-1, keepdims=True), sink)
        p = jnp.exp(s - m)
        denom = jnp.sum(p, axis=-1, keepdims=True) + jnp.exp(sink - m)
        o = _dot(p.astype(BF16), vband) * (1.0 / denom)
        for h2 in range(GROUP_SIZE // 2):
            even = o[(2 * h2) * blk:(2 * h2 + 1) * blk]
            odd = o[(2 * h2 + 1) * blk:(2 * h2 + 2) * blk]
            pair = g * (GROUP_SIZE // 2) + h2
            o_ref[:, pair * LANES:(pair + 1) * LANES] = jnp.where(lo, even, odd).astype(BF16)


def _attention(q, kd, vd, sinks, batch, seq):
    n, d_q = q.shape
    blk = ATT_BLOCK
    nb = seq // blk
    w_kv = kd.shape[1]
    cur = lambda b, i: (b * nb + i, 0)
    prev = lambda b, i: (b * nb + jnp.maximum(i - 1, 0), 0)
    return pl.pallas_call(
        _attn_kernel,
        grid=(batch, nb),
        in_specs=[
            pl.BlockSpec(memory_space=pltpu.SMEM),
            pl.BlockSpec((blk, d_q), cur),
            pl.BlockSpec((blk, w_kv), cur),
            pl.BlockSpec((blk, w_kv), prev),
            pl.BlockSpec((blk, w_kv), cur),
            pl.BlockSpec((blk, w_kv), prev),
        ],
        out_specs=pl.BlockSpec((blk, d_q), cur),
        out_shape=jax.ShapeDtypeStruct((n, d_q), BF16),
        compiler_params=_params("arbitrary", "arbitrary"),
        name="banded_sink_attention",
    )(sinks.astype(F32), q, kd, kd, vd, vd)


def _oproj_kernel(a_ref, x_ref, w_ref, gn_ref, xo_ref, hn_ref):
    out = x_ref[...] + _dot(a_ref[...], w_ref[...])
    xo_ref[...] = out
    hn_ref[...] = _rms_scale(out, gn_ref[...]).astype(BF16)


def _oproj(a, x, w, g_next, tm=512):
    n, d = x.shape
    k = a.shape[1]
    return pl.pallas_call(
        _oproj_kernel,
        grid=(n // tm,),
        in_specs=[
            pl.BlockSpec((tm, k), lambda i: (i, 0)),
            pl.BlockSpec((tm, d), lambda i: (i, 0)),
            pl.BlockSpec((k, d), lambda i: (0, 0)),
            pl.BlockSpec((1, d), lambda i: (0, 0)),
        ],
        out_specs=[pl.BlockSpec((tm, d), lambda i: (i, 0)), pl.BlockSpec((tm, d), lambda i: (i, 0))],
        out_shape=[jax.ShapeDtypeStruct((n, d), F32), jax.ShapeDtypeStruct((n, d), BF16)],
        compiler_params=_params("arbitrary"),
        name="attn_out_proj",
    )(a, x, w, _row(g_next))


def kernel(x, positions, l0_norm_g, l0_a_w_in, l0_a_b_in, l0_a_dw_w, l0_a_dw_b, l0_a_ln_g, l0_a_ln_b, l0_a_w_out, l0_a_b_out, l0_ffn_norm_g, l0_ffn_w_up, l0_ffn_dw_w, l0_ffn_dw_b, l0_ffn_w_down, l1_norm_g, l1_b_w_group, l1_b_scale, l1_ffn_norm_g, l1_ffn_w_up, l1_ffn_dw_w, l1_ffn_dw_b, l1_ffn_w_down, l2_norm_g, l2_c_w_qkv, l2_c_q_norm_g, l2_c_k_norm_g, l2_c_sinks, l2_c_w_o, l2_ffn_norm_g, l2_ffn_w_up, l2_ffn_dw_w, l2_ffn_dw_b, l2_ffn_w_down, l3_norm_g, l3_a_w_in, l3_a_b_in, l3_a_dw_w, l3_a_dw_b, l3_a_ln_g, l3_a_ln_b, l3_a_w_out, l3_a_b_out, l3_ffn_norm_g, l3_ffn_w_up, l3_ffn_dw_w, l3_ffn_dw_b, l3_ffn_w_down):
    batch, seq, d = x.shape
    n = batch * seq
    xs = x.reshape(n, d)
    bf = lambda w: w.astype(BF16)

    def conformer(xs, hn, w_in, b_in, dw_w, dw_b, ln_g, ln_b, w_out, b_out, g_next):
        glu = _conformer_glu(hn, bf(w_in), b_in)
        return _conformer_out(glu, xs, dw_w, dw_b, ln_g, ln_b, bf(w_out), b_out, g_next, seq)

    def ffn(xs, hn, w_up, dw_w, dw_b, w_down, g_next):
        return _ffn(hn, xs, bf(w_up), dw_w, dw_b, bf(w_down), g_next, seq)

    hn = _prenorm(xs, l0_norm_g)
    xs, hn = conformer(xs, hn, l0_a_w_in, l0_a_b_in, l0_a_dw_w, l0_a_dw_b, l0_a_ln_g, l0_a_ln_b, l0_a_w_out,
                       l0_a_b_out, l0_ffn_norm_g)
    xs, hn = ffn(xs, hn, l0_ffn_w_up, l0_ffn_dw_w, l0_ffn_dw_b, l0_ffn_w_down, l1_norm_g)
    xs, hn = _pool_mixer(hn, xs, bf(l1_b_w_group), l1_b_scale, l1_ffn_norm_g, seq)
    xs, hn = ffn(xs, hn, l1_ffn_w_up, l1_ffn_dw_w, l1_ffn_dw_b, l1_ffn_w_down, l2_norm_g)
    q, kd, vd = _qkv(hn, positions, bf(l2_c_w_qkv), l2_c_q_norm_g, l2_c_k_norm_g)
    att = _attention(q, kd, vd, l2_c_sinks, batch, seq)
    xs, hn = _oproj(att, xs, bf(l2_c_w_o), l2_ffn_norm_g)
    xs, hn = ffn(xs, hn, l2_ffn_w_up, l2_ffn_dw_w, l2_ffn_dw_b, l2_ffn_w_down, l3_norm_g)
    xs, hn = conformer(xs, hn, l3_a_w_in, l3_a_b_in, l3_a_dw_w, l3_a_dw_b, l3_a_ln_g, l3_a_ln_b, l3_a_w_out,
                       l3_a_b_out, l3_ffn_norm_g)
    xs, _ = ffn(xs, hn, l3_ffn_w_up, l3_ffn_dw_w, l3_ffn_dw_b, l3_ffn_w_down, None)
    return xs.reshape(batch, seq, d)
```

```python
import functools

import jax
import jax.numpy as jnp
from jax import lax
from jax.experimental import pallas as pl
from jax.experimental.pallas import tpu as pltpu

F32 = jnp.float32
BF16 = jnp.bfloat16

EPS = 1e-6
LANES = 128
HEAD_DIM = 64
N_HEADS = 32
N_KV_HEADS = 4
GROUP_SIZE = N_HEADS // N_KV_HEADS
ATT_BLOCK = 128
ROT_DIM = 16
ROPE_THETA = 500000.0
POOL_WINDOWS = (2, 4, 8, 16)
CONF_WIDTH = 31
FFN_CONV_WIDTH = 3
CONF_HALO = 32
POOL_HALO = 16
FFN_HALO = 8
NEG_BIG = -1e30

VMEM_LIMIT_BYTES = 56 * 1024 * 1024


def _params(*semantics):
    return pltpu.CompilerParams(dimension_semantics=semantics, vmem_limit_bytes=VMEM_LIMIT_BYTES)


def _sigmoid(x):
    return 1.0 / (1.0 + jnp.exp(-x))


def _rms_scale(x, g):
    ms = jnp.mean(x * x, axis=-1, keepdims=True)
    return x * lax.rsqrt(ms + EPS) * g


def _dot(a, b):
    return jnp.dot(a, b, preferred_element_type=F32)


def _row(v):
    return v.reshape(1, -1)


def _glu_kernel(act_ref, gn_ref, wa_ref, wg_ref, ba_ref, bg_ref, o_ref, *scratch, fuse_norm):
    if fuse_norm:
        (hn_ref,) = scratch

        @pl.when(pl.program_id(1) == 0)
        def _():
            hn_ref[...] = _rms_scale(act_ref[...], gn_ref[...]).astype(BF16)
    else:
        hn_ref = act_ref
    hn = hn_ref[...]
    a = _dot(hn, wa_ref[...]) + ba_ref[...]
    g = _dot(hn, wg_ref[...]) + bg_ref[...]
    o_ref[...] = (a * _sigmoid(g)).astype(BF16)


def _conformer_glu(act, w_in, b_in, norm_g=None, tm=1024, tn=1024):
    n, d = act.shape
    nb = d // tn
    b_in = _row(b_in)
    fuse_norm = norm_g is not None
    if not fuse_norm:
        norm_g = jnp.ones((d,), F32)
    return pl.pallas_call(
        functools.partial(_glu_kernel, fuse_norm=fuse_norm),
        grid=(n // tm, nb),
        in_specs=[
            pl.BlockSpec((tm, d), lambda i, j: (i, 0)),
            pl.BlockSpec((1, d), lambda i, j: (0, 0)),
            pl.BlockSpec((d, tn), lambda i, j: (0, j)),
            pl.BlockSpec((d, tn), lambda i, j: (0, j + nb)),
            pl.BlockSpec((1, tn), lambda i, j: (0, j)),
            pl.BlockSpec((1, tn), lambda i, j: (0, j + nb)),
        ],
        out_specs=pl.BlockSpec((tm, tn), lambda i, j: (i, j)),
        out_shape=jax.ShapeDtypeStruct((n, d), BF16),
        scratch_shapes=[pltpu.VMEM((tm, d), BF16)] if fuse_norm else [],
        compiler_params=_params("arbitrary", "arbitrary"),
        name="conformer_glu",
    )(act, _row(norm_g), w_in, w_in, b_in, b_in)


def _conformer_out_kernel(glu_ref, halo_ref, x_ref, dww_ref, dwb_ref, lng_ref, lnb_ref, wo_ref, bo_ref, gn_ref,
                          xo_ref, hn_ref, s_ref, y_ref, *, tiles_per_seq, row_chunk):
    tm, d = glu_ref.shape
    ncol = d // LANES
    first = (pl.program_id(0) % tiles_per_seq) == 0
    halo = jnp.where(first, 0.0, halo_ref[...].astype(F32))
    cur = glu_ref[...].astype(F32)
    for c in range(ncol):
        s_ref[c, 0:CONF_HALO, :] = halo[:, c * LANES:(c + 1) * LANES]
        s_ref[c, CONF_HALO:CONF_HALO + tm, :] = cur[:, c * LANES:(c + 1) * LANES]

    base = CONF_HALO - (CONF_WIDTH - 1)

    def col_body(c, carry):
        bias = dwb_ref[c]
        for r in range(tm // row_chunk):
            acc = jnp.broadcast_to(bias, (row_chunk, LANES))
            for j in range(CONF_WIDTH):
                acc = acc + dww_ref[c, j:j + 1, :] * s_ref[c, pl.ds(r * row_chunk + base + j, row_chunk), :]
            y_ref[c, r * row_chunk:(r + 1) * row_chunk, :] = acc
        return carry

    lax.fori_loop(0, ncol, col_body, 0)

    y = jnp.concatenate([y_ref[c] for c in range(ncol)], axis=1)
    mu = jnp.mean(y, axis=-1, keepdims=True)
    yc = y - mu
    var = jnp.mean(yc * yc, axis=-1, keepdims=True)
    z = yc * lax.rsqrt(var + EPS) * lng_ref[...] + lnb_ref[...]
    a = (z * _sigmoid(z)).astype(BF16)
    out = x_ref[...] + (_dot(a, wo_ref[...]) + bo_ref[...])
    xo_ref[...] = out
    hn_ref[...] = _rms_scale(out, gn_ref[...]).astype(BF16)


def _conformer_out(glu, x, dw_w, dw_b, ln_g, ln_b, w_out, b_out, g_next, seq, tm=512, row_chunk=64):
    n, d = x.shape
    ncol = d // LANES
    halo_blocks = tm // CONF_HALO
    dww = dw_w.reshape(CONF_WIDTH, ncol, LANES).transpose(1, 0, 2)
    dwb = dw_b.reshape(ncol, 1, LANES)
    kern = functools.partial(_conformer_out_kernel, tiles_per_seq=seq // tm, row_chunk=row_chunk)
    vec = pl.BlockSpec((1, d), lambda i: (0, 0))
    return pl.pallas_call(
        kern,
        grid=(n // tm,),
        in_specs=[
            pl.BlockSpec((tm, d), lambda i: (i, 0)),
            pl.BlockSpec((CONF_HALO, d), lambda i: (jnp.maximum(i * halo_blocks - 1, 0), 0)),
            pl.BlockSpec((tm, d), lambda i: (i, 0)),
            pl.BlockSpec((ncol, CONF_WIDTH, LANES), lambda i: (0, 0, 0)),
            pl.BlockSpec((ncol, 1, LANES), lambda i: (0, 0, 0)),
            vec, vec,
            pl.BlockSpec((d, d), lambda i: (0, 0)),
            vec, vec,
        ],
        out_specs=[pl.BlockSpec((tm, d), lambda i: (i, 0)), pl.BlockSpec((tm, d), lambda i: (i, 0))],
        out_shape=[jax.ShapeDtypeStruct((n, d), F32), jax.ShapeDtypeStruct((n, d), BF16)],
        scratch_shapes=[pltpu.VMEM((ncol, tm + CONF_HALO, LANES), F32), pltpu.VMEM((ncol, tm, LANES), F32)],
        compiler_params=_params("arbitrary"),
        name="conformer_out",
    )(glu, glu, x, dww, dwb, _row(ln_g), _row(ln_b), w_out, _row(b_out), _row(g_next))


def _ffn_kernel(hn_ref, x_hbm, wg_ref, wv_ref, cwg_ref, cwv_ref, cbg_ref, cbv_ref, wd_ref, gn_ref, *rest,
                tiles_per_seq, emit_hn):
    if emit_hn:
        xo_ref, hn_out_ref, p_even_ref, p_odd_ref, carry_g_ref, carry_v_ref, x_sem = rest
    else:
        xo_ref, p_even_ref, p_odd_ref, carry_g_ref, carry_v_ref, x_sem = rest
    p_refs = (p_even_ref, p_odd_ref)
    tm = hn_ref.shape[0]
    i = pl.program_id(0)
    j = pl.program_id(1)
    nj = pl.num_programs(1) - 1
    first = (i % tiles_per_seq) == 0

    def x_copy():
        return pltpu.make_async_copy(x_hbm.at[pl.ds(pl.multiple_of(i * tm, tm), tm), :], xo_ref, x_sem)

    def shifted(u, carry, k):
        r = pltpu.roll(u, k, 0)
        row = lax.broadcasted_iota(jnp.int32, (FFN_HALO, 1), 0)
        head = jnp.where(row < k, pltpu.roll(carry, k, 0), r[0:FFN_HALO])
        return jnp.concatenate([head, r[FFN_HALO:]], axis=0)

    def causal_conv(u, carry_ref, cw_ref, cb_ref):
        carry = jnp.where(first, 0.0, carry_ref[j])
        carry_ref[j] = u[tm - FFN_HALO:, :]
        out = cb_ref[...] + cw_ref[FFN_CONV_WIDTH - 1:FFN_CONV_WIDTH, :] * u
        for k in range(1, FFN_CONV_WIDTH):
            out = out + cw_ref[FFN_CONV_WIDTH - 1 - k:FFN_CONV_WIDTH - k, :] * shifted(u, carry, k)
        return out

    def up(p_ref):
        hn = hn_ref[...]
        gate = causal_conv(_dot(hn, wg_ref[...]), carry_g_ref, cwg_ref, cbg_ref)
        val = causal_conv(_dot(hn, wv_ref[...]), carry_v_ref, cwv_ref, cbv_ref)
        p_ref[...] = (gate * _sigmoid(gate) * val).astype(BF16)

    def down(parity):
        xo_ref[...] += _dot(p_refs[parity][...], wd_ref[...])

    @pl.when(j == 0)
    def _():
        x_copy().start()
        up(p_refs[0])

    @pl.when(j == 1)
    def _():
        x_copy().wait()

    for parity in (0, 1):
        @pl.when((j > 0) & (j < nj) & (j % 2 == parity))
        def _():
            up(p_refs[parity])
            down(1 - parity)

    @pl.when(j == nj)
    def _():
        down((nj - 1) % 2)
        if emit_hn:
            hn_out_ref[...] = _rms_scale(xo_ref[...], gn_ref[...]).astype(BF16)


def _ffn(hn, x, w_up, dw_w, dw_b, w_down, g_next, seq, tm=1024, tf=512):
    n, d = x.shape
    dff = w_down.shape[0]
    nj = dff // tf
    dw_b = _row(dw_b)
    emit_hn = g_next is not None
    if not emit_hn:
        g_next = jnp.ones((d,), F32)
    kern = functools.partial(_ffn_kernel, tiles_per_seq=seq // tm, emit_hn=emit_hn)
    row_tile = pl.BlockSpec((tm, d), lambda i, j: (i, 0))
    up_blk = lambda i, j: (0, jnp.minimum(j, nj - 1))
    up_blk_val = lambda i, j: (0, jnp.minimum(j, nj - 1) + nj)
    n_out = 2 if emit_hn else 1
    res = pl.pallas_call(
        kern,
        grid=(n // tm, nj + 1),
        in_specs=[
            row_tile,
            pl.BlockSpec(memory_space=pl.ANY),
            pl.BlockSpec((d, tf), up_blk),
            pl.BlockSpec((d, tf), up_blk_val),
            pl.BlockSpec((FFN_CONV_WIDTH, tf), up_blk),
            pl.BlockSpec((FFN_CONV_WIDTH, tf), up_blk_val),
            pl.BlockSpec((1, tf), up_blk),
            pl.BlockSpec((1, tf), up_blk_val),
            pl.BlockSpec((tf, d), lambda i, j: (jnp.maximum(j - 1, 0), 0)),
            pl.BlockSpec((1, d), lambda i, j: (0, 0)),
        ],
        out_specs=[row_tile, row_tile][:n_out],
        out_shape=[jax.ShapeDtypeStruct((n, d), F32), jax.ShapeDtypeStruct((n, d), BF16)][:n_out],
        scratch_shapes=[
            pltpu.VMEM((tm, tf), BF16),
            pltpu.VMEM((tm, tf), BF16),
            pltpu.VMEM((nj, FFN_HALO, tf), F32),
            pltpu.VMEM((nj, FFN_HALO, tf), F32),
            pltpu.SemaphoreType.DMA(()),
        ],
        compiler_params=_params("arbitrary", "arbitrary"),
        name="conv_gated_mlp",
    )(hn, x, w_up, w_up, dw_w, dw_w, dw_b, dw_b, w_down, _row(g_next))
    return (res[0], res[1]) if emit_hn else (res[0], None)


def _pool_kernel(hn_ref, halo_ref, x_ref, wgrp_ref, scale_ref, gn_ref, xo_ref, hn_out_ref, s_ref, m_ref,
                 *, tiles_per_seq, row_chunk):
    tm, d = hn_ref.shape
    gdim = d // len(POOL_WINDOWS)
    tile = pl.program_id(0) % tiles_per_seq
    first = tile == 0
    halo = jnp.where(first, 0.0, halo_ref[...].astype(F32))
    cur_tile = hn_ref[...].astype(F32)
    for c in range(d // LANES):
        s_ref[c, 0:POOL_HALO, :] = halo[:, c * LANES:(c + 1) * LANES]
        s_ref[c, POOL_HALO:POOL_HALO + tm, :] = cur_tile[:, c * LANES:(c + 1) * LANES]

    t = tile * tm + lax.broadcasted_iota(jnp.int32, (tm, 1), 0)
    for g, w in enumerate(POOL_WINDOWS):
        inv_count = 1.0 / jnp.minimum(t + 1, w).astype(F32)
        for c in range(g * gdim // LANES, (g + 1) * gdim // LANES):
            cols = slice(c * LANES, (c + 1) * LANES)
            for r in range(tm // row_chunk):
                rows = r * row_chunk
                cur = s_ref[c, POOL_HALO + rows:POOL_HALO + rows + row_chunk, :]
                acc = cur
                for k in range(1, w):
                    acc = acc + s_ref[c, pl.ds(POOL_HALO + rows - k, row_chunk), :]
                mixed = acc * inv_count[rows:rows + row_chunk] - cur
                m_ref[rows:rows + row_chunk, cols] = mixed.astype(BF16)

    y = jnp.concatenate(
        [_dot(m_ref[:, g * gdim:(g + 1) * gdim], wgrp_ref[g]) for g in range(len(POOL_WINDOWS))], axis=1)
    out = x_ref[...] + y * scale_ref[...]
    xo_ref[...] = out
    hn_out_ref[...] = _rms_scale(out, gn_ref[...]).astype(BF16)


def _pool_mixer(hn, x, w_group, scale, g_next, seq, tm=512, row_chunk=64):
    n, d = x.shape
    ng, gdim, _ = w_group.shape
    halo_blocks = tm // POOL_HALO
    kern = functools.partial(_pool_kernel, tiles_per_seq=seq // tm, row_chunk=row_chunk)
    vec = pl.BlockSpec((1, d), lambda i: (0, 0))
    return pl.pallas_call(
        kern,
        grid=(n // tm,),
        in_specs=[
            pl.BlockSpec((tm, d), lambda i: (i, 0)),
            pl.BlockSpec((POOL_HALO, d), lambda i: (jnp.maximum(i * halo_blocks - 1, 0), 0)),
            pl.BlockSpec((tm, d), lambda i: (i, 0)),
            pl.BlockSpec((ng, gdim, gdim), lambda i: (0, 0, 0)),
            vec, vec,
        ],
        out_specs=[pl.BlockSpec((tm, d), lambda i: (i, 0)), pl.BlockSpec((tm, d), lambda i: (i, 0))],
        out_shape=[jax.ShapeDtypeStruct((n, d), F32), jax.ShapeDtypeStruct((n, d), BF16)],
        scratch_shapes=[pltpu.VMEM((d // LANES, tm + POOL_HALO, LANES), F32), pltpu.VMEM((tm, d), BF16)],
        compiler_params=_params("arbitrary"),
        name="pool_mixer",
    )(hn, hn, x, w_group, _row(scale), _row(g_next))


def _qkv_kernel(hn_ref, pos_ref, w_ref, e_ref, et_ref, gain_ref, invf_ref, q_ref, k_ref, v_ref, *, sub_tiles):
    tm, d_q = q_ref.shape
    d_kv = N_KV_HEADS * HEAD_DIM
    d_qk = d_q + d_kv
    half = ROT_DIM // 2
    lane = lax.broadcasted_iota(jnp.int32, (1, LANES), 1) % HEAD_DIM
    lo = lax.broadcasted_iota(jnp.int32, (1, LANES), 1) < HEAD_DIM
    nq = d_q // LANES
    sm_scale = 1.0 / (HEAD_DIM ** 0.5)

    def dup(xc):
        r = pltpu.roll(xc, HEAD_DIM, 1)
        return [jnp.where(lo, xc, r), jnp.where(lo, r, xc)]

    sub = tm // sub_tiles
    for m in range(sub_tiles):
        rows = slice(m * sub, (m + 1) * sub)
        acc = _dot(hn_ref[rows, :], w_ref[...])
        qk = acc[:, :d_qk]

        ms = _dot((qk * qk).astype(BF16), e_ref[...]) * (1.0 / HEAD_DIM)
        scale = lax.rsqrt(ms + EPS)
        scale_hi = scale.astype(BF16)
        scale_lo = (scale - scale_hi.astype(F32)).astype(BF16)
        scale_b = _dot(scale_hi, et_ref[...]) + _dot(scale_lo, et_ref[...])
        qkn = qk * scale_b * gain_ref[...]

        ang = pos_ref[rows, :].astype(F32) * invf_ref[...]
        cos = jnp.cos(ang)
        sin = jnp.sin(ang)
        c_self = jnp.where(lane < ROT_DIM, cos, 1.0)
        c_up = jnp.where(lane < half, -sin, 0.0)
        c_dn = jnp.where((lane >= half) & (lane < ROT_DIM), sin, 0.0)
        cols = []
        for c in range(d_qk // LANES):
            xc = qkn[:, c * LANES:(c + 1) * LANES]
            cols.append(xc * c_self + pltpu.roll(xc, LANES - half, 1) * c_up + pltpu.roll(xc, half, 1) * c_dn)

        q_ref[rows, :] = (jnp.concatenate(cols[:nq], axis=1) * sm_scale).astype(BF16)

        kd, vd = [], []
        for c in range(d_kv // LANES):
            kd += dup(cols[nq + c])
            vd += dup(acc[:, d_qk + c * LANES:d_qk + (c + 1) * LANES])
        k_ref[rows, :] = jnp.concatenate(kd, axis=1).astype(BF16)
        v_ref[rows, :] = jnp.concatenate(vd, axis=1).astype(BF16)


def _qkv(hn, positions, w_qkv, q_gain, k_gain, tm=1024, sub_tiles=4):
    n, d = hn.shape
    d_q = N_HEADS * HEAD_DIM
    d_kv = N_KV_HEADS * HEAD_DIM
    d_qk = d_q + d_kv
    d_all = d_q + 2 * d_kv
    head_of_col = jnp.arange(d_qk) // HEAD_DIM
    e = (head_of_col[:, None] == jnp.arange(LANES)[None, :]).astype(BF16)
    et = e.T
    gain = jnp.concatenate([jnp.tile(q_gain, N_HEADS), jnp.tile(k_gain, N_KV_HEADS)]).astype(F32)
    inv_freq = ROPE_THETA ** (-jnp.arange(0, ROT_DIM, 2, dtype=F32) / ROT_DIM)
    lane = jnp.arange(LANES) % HEAD_DIM
    invf = jnp.where(lane < ROT_DIM, inv_freq[lane % (ROT_DIM // 2)], 0.0).astype(F32)
    const = lambda shape: pl.BlockSpec(shape, lambda i: (0, 0), pipeline_mode=pl.Buffered(1))
    return pl.pallas_call(
        functools.partial(_qkv_kernel, sub_tiles=sub_tiles),
        grid=(n // tm,),
        in_specs=[
            pl.BlockSpec((tm, d), lambda i: (i, 0)),
            pl.BlockSpec((tm, 1), lambda i: (i, 0)),
            const((d, d_all)), const((d_qk, LANES)), const((LANES, d_qk)), const((1, d_qk)), const((1, LANES)),
        ],
        out_specs=[
            pl.BlockSpec((tm, d_q), lambda i: (i, 0)),
            pl.BlockSpec((tm, 2 * d_kv), lambda i: (i, 0)),
            pl.BlockSpec((tm, 2 * d_kv), lambda i: (i, 0)),
        ],
        out_shape=[
            jax.ShapeDtypeStruct((n, d_q), BF16),
            jax.ShapeDtypeStruct((n, 2 * d_kv), BF16),
            jax.ShapeDtypeStruct((n, 2 * d_kv), BF16),
        ],
        compiler_params=_params("arbitrary"),
        name="qkv_norm_rope",
    )(hn, positions.reshape(n, 1), w_qkv, e, et, _row(gain), _row(invf))


def _attn_kernel(sinks_ref, q_ref, kc_ref, kp_ref, vc_ref, vp_ref, o_ref):
    blk = q_ref.shape[0]
    n = pl.program_id(1)
    lo = lax.broadcasted_iota(jnp.int32, (1, LANES), 1) < HEAD_DIM
    rows = GROUP_SIZE * blk
    qi = lax.broadcasted_iota(jnp.int32, (rows, 2 * blk), 0) % blk
    kj = lax.broadcasted_iota(jnp.int32, (rows, 2 * blk), 1)
    valid = (kj > qi) & (kj <= qi + blk) & ((kj >= blk) | (n > 0))
    bias = jnp.where(valid, 0.0, NEG_BIG)
    contract_last = (((1,), (1,)), ((), ()))
    for g in range(N_KV_HEADS):
        gcols = slice(g * LANES, (g + 1) * LANES)
        kband = jnp.concatenate([kp_ref[:, gcols], kc_ref[:, gcols]], axis=0)
        vband = jnp.concatenate([vp_ref[:, gcols], vc_ref[:, gcols]], axis=0)
        parts, sink_parts = [], []
        for h in range(GROUP_SIZE):
            pair = g * (GROUP_SIZE // 2) + h // 2
            qp = q_ref[:, pair * LANES:(pair + 1) * LANES]
            keep = lo if h % 2 == 0 else jnp.logical_not(lo)
            parts.append(jnp.where(keep, qp, jnp.zeros_like(qp)))
            sink_parts.append(jnp.full((blk, 1), sinks_ref[g * GROUP_SIZE + h], F32))
        qs = jnp.concatenate(parts, axis=0)
        sink = jnp.concatenate(sink_parts, axis=0)
        s = lax.dot_general(qs, kband, contract_last, preferred_element_type=F32) + bias
        m = jnp.maximum(jnp.max(s, axis=-1, keepdims=True), sink)
        p = jnp.exp(s - m)
        denom = jnp.sum(p, axis=-1, keepdims=True) + jnp.exp(sink - m)
        o = _dot(p.astype(BF16), vband) * (1.0 / denom)
        for h2 in range(GROUP_SIZE // 2):
            even = o[(2 * h2) * blk:(2 * h2 + 1) * blk]
            odd = o[(2 * h2 + 1) * blk:(2 * h2 + 2) * blk]
            pair = g * (GROUP_SIZE // 2) + h2
            o_ref[:, pair * LANES:(pair + 1) * LANES] = jnp.where(lo, even, odd).astype(BF16)


def _attention(q, kd, vd, sinks, batch, seq):
    n, d_q = q.shape
    blk = ATT_BLOCK
    nb = seq // blk
    w_kv = kd.shape[1]
    cur = lambda b, i: (b * nb + i, 0)
    prev = lambda b, i: (b * nb + jnp.maximum(i - 1, 0), 0)
    return pl.pallas_call(
        _attn_kernel,
        grid=(batch, nb),
        in_specs=[
            pl.BlockSpec(memory_space=pltpu.SMEM),
            pl.BlockSpec((blk, d_q), cur),
            pl.BlockSpec((blk, w_kv), cur),
            pl.BlockSpec((blk, w_kv), prev),
            pl.BlockSpec((blk, w_kv), cur),
            pl.BlockSpec((blk, w_kv), prev),
        ],
        out_specs=pl.BlockSpec((blk, d_q), cur),
        out_shape=jax.ShapeDtypeStruct((n, d_q), BF16),
        compiler_params=_params("arbitrary", "arbitrary"),
        name="banded_sink_attention",
    )(sinks.astype(F32), q, kd, kd, vd, vd)


def _oproj_kernel(a_ref, x_ref, w_ref, gn_ref, xo_ref, hn_ref):
    out = x_ref[...] + _dot(a_ref[...], w_ref[...])
    xo_ref[...] = out
    hn_ref[...] = _rms_scale(out, gn_ref[...]).astype(BF16)


def _oproj(a, x, w, g_next, tm=512):
    n, d = x.shape
    k = a.shape[1]
    return pl.pallas_call(
        _oproj_kernel,
        grid=(n // tm,),
        in_specs=[
            pl.BlockSpec((tm, k), lambda i: (i, 0)),
            pl.BlockSpec((tm, d), lambda i: (i, 0)),
            pl.BlockSpec((k, d), lambda i: (0, 0)),
            pl.BlockSpec((1, d), lambda i: (0, 0)),
        ],
        out_specs=[pl.BlockSpec((tm, d), lambda i: (i, 0)), pl.BlockSpec((tm, d), lambda i: (i, 0))],
        out_shape=[jax.ShapeDtypeStruct((n, d), F32), jax.ShapeDtypeStruct((n, d), BF16)],
        compiler_params=_params("arbitrary"),
        name="attn_out_proj",
    )(a, x, w, _row(g_next))


def kernel(x, positions, l0_norm_g, l0_a_w_in, l0_a_b_in, l0_a_dw_w, l0_a_dw_b, l0_a_ln_g, l0_a_ln_b, l0_a_w_out, l0_a_b_out, l0_ffn_norm_g, l0_ffn_w_up, l0_ffn_dw_w, l0_ffn_dw_b, l0_ffn_w_down, l1_norm_g, l1_b_w_group, l1_b_scale, l1_ffn_norm_g, l1_ffn_w_up, l1_ffn_dw_w, l1_ffn_dw_b, l1_ffn_w_down, l2_norm_g, l2_c_w_qkv, l2_c_q_norm_g, l2_c_k_norm_g, l2_c_sinks, l2_c_w_o, l2_ffn_norm_g, l2_ffn_w_up, l2_ffn_dw_w, l2_ffn_dw_b, l2_ffn_w_down, l3_norm_g, l3_a_w_in, l3_a_b_in, l3_a_dw_w, l3_a_dw_b, l3_a_ln_g, l3_a_ln_b, l3_a_w_out, l3_a_b_out, l3_ffn_norm_g, l3_ffn_w_up, l3_ffn_dw_w, l3_ffn_dw_b, l3_ffn_w_down):
    batch, seq, d = x.shape
    n = batch * seq
    xs = x.reshape(n, d)
    bf = lambda w: w.astype(BF16)

    def conformer(xs, hn, w_in, b_in, dw_w, dw_b, ln_g, ln_b, w_out, b_out, g_next, norm_g=None):
        glu = _conformer_glu(xs, bf(w_in), b_in, norm_g) if hn is None else _conformer_glu(hn, bf(w_in), b_in)
        return _conformer_out(glu, xs, dw_w, dw_b, ln_g, ln_b, bf(w_out), b_out, g_next, seq)

    def ffn(xs, hn, w_up, dw_w, dw_b, w_down, g_next):
        return _ffn(hn, xs, bf(w_up), dw_w, dw_b, bf(w_down), g_next, seq)

    xs, hn = conformer(xs, None, l0_a_w_in, l0_a_b_in, l0_a_dw_w, l0_a_dw_b, l0_a_ln_g, l0_a_ln_b, l0_a_w_out,
                       l0_a_b_out, l0_ffn_norm_g, norm_g=l0_norm_g)
    xs, hn = ffn(xs, hn, l0_ffn_w_up, l0_ffn_dw_w, l0_ffn_dw_b, l0_ffn_w_down, l1_norm_g)
    xs, hn = _pool_mixer(hn, xs, bf(l1_b_w_group), l1_b_scale, l1_ffn_norm_g, seq)
    xs, hn = ffn(xs, hn, l1_ffn_w_up, l1_ffn_dw_w, l1_ffn_dw_b, l1_ffn_w_down, l2_norm_g)
    q, kd, vd = _qkv(hn, positions, bf(l2_c_w_qkv), l2_c_q_norm_g, l2_c_k_norm_g)
    att = _attention(q, kd, vd, l2_c_sinks, batch, seq)
    xs, hn = _oproj(att, xs, bf(l2_c_w_o), l2_ffn_norm_g)
    xs, hn = ffn(xs, hn, l2_ffn_w_up, l2_ffn_dw_w, l2_ffn_dw_b, l2_ffn_w_down, l3_norm_g)
    xs, hn = conformer(xs, hn, l3_a_w_in, l3_a_b_in, l3_a_dw_w, l3_a_dw_b, l3_a_ln_g, l3_a_ln_b, l3_a_w_out,
                       l3_a_b_out, l3_ffn_norm_g)
    xs, _ = ffn(xs, hn, l3_ffn_w_up, l3_ffn_dw_w, l3_ffn_dw_b, l3_ffn_w_down, None)
    return xs.reshape(batch, seq, d)
```
